```python
import jax, jax.numpy as jnp
from jax import lax
import numpy as np

D_MODEL = 1024
BATCH = 4
SEQ = 4096
DEPTH = 2
DEC_BATCH = 128
DEC_SEQ = 4
PAST_LEN = 8192
PAGE_SIZE = 128

N_META = 16
HEAD_DIM = 64
ROPE_THETA = 10000.0
NORM_EPS = 1e-6
Q_BLOCK = 128
N_EVEN = (DEPTH + 1) // 2
N_ODD = DEPTH // 2
MLA_HEADS = 8
MLA_NOPE = 64
MLA_ROPE = 32
MLA_QK = MLA_NOPE + MLA_ROPE
MLA_V = 64
MLA_Q_LORA = 384
MLA_KV_LORA = 256
MLA_LAT = MLA_KV_LORA + MLA_ROPE
MLA_IN = MLA_Q_LORA + MLA_KV_LORA + MLA_ROPE
DSA_HEADS = 8
DSA_KV_HEADS = 2
DSA_GROUP = DSA_HEADS // DSA_KV_HEADS
IDX_HEADS = 8
IDX_DIM = 64
TOPK_MAX = 256
IDX_SCALE = (IDX_DIM * IDX_HEADS) ** -0.5
DSA_SIZES = (DSA_HEADS * HEAD_DIM, DSA_KV_HEADS * HEAD_DIM, DSA_KV_HEADS * HEAD_DIM, IDX_HEADS * IDX_DIM, IDX_HEADS, IDX_DIM)
EVEN_IN = MLA_IN + sum(DSA_SIZES)
EVEN_MIX = MLA_HEADS * MLA_V + DSA_HEADS * HEAD_DIM
FOX_HEADS = 16
FOX_KV_HEADS = 8
FOX_GROUP = FOX_HEADS // FOX_KV_HEADS
FOX_SIZES = (FOX_HEADS * HEAD_DIM, FOX_KV_HEADS * HEAD_DIM, FOX_KV_HEADS * HEAD_DIM, FOX_HEADS)
ODD_IN = sum(FOX_SIZES)
ODD_MIX = FOX_HEADS * HEAD_DIM
D_FF = 256 * (-(-8 * D_MODEL // (3 * 256)))

kernel_name = 'hybrid_mla_dsa_fox_meta_step'


def rms_norm(x, g):
    xf = x.astype(jnp.float32)
    y = xf * lax.rsqrt(jnp.mean(xf * xf, axis=-1, keepdims=True) + NORM_EPS)
    return (y * g.astype(jnp.float32)).astype(x.dtype)


def rope(x, pos):
    half = x.shape[-1] // 2
    inv = ROPE_THETA ** (-jnp.arange(half, dtype=jnp.float32) / half)
    ang = pos.astype(jnp.float32)[:, None] * inv
    cos, sin = jnp.cos(ang)[:, None, :], jnp.sin(ang)[:, None, :]
    xf = x.astype(jnp.float32)
    x1, x2 = xf[..., :half], xf[..., half:]
    return jnp.concatenate([x1 * cos - x2 * sin, x2 * cos + x1 * sin], axis=-1).astype(x.dtype)


def split_cols(p, sizes):
    out, o = [], 0
    for s in sizes:
        out.append(p[..., o:o + s])
        o += s
    return out


def gather_pages(pool, li, pages):
    rows = pool[li, pages]
    return rows.reshape((-1,) + pool.shape[3:])


def softmax_attend(q, k, v, mask, bias=None):
    s = jnp.einsum('bqhgd,bkhd->bhgqk', q, k).astype(jnp.float32) * (q.shape[-1] ** -0.5)
    if bias is not None:
        s = s + bias
    s = jnp.where(mask, s, -jnp.inf)
    p = jax.nn.softmax(s, axis=-1).astype(v.dtype)
    return jnp.einsum('bhgqk,bkhd->bqhgd', p, v)


def dsa_attend(q, iq, iw, k, v, ik, q_pos, k_pos, topk):
    dots = jnp.einsum('bqhd,bkd->bqhk', iq, ik).astype(jnp.float32)
    score = jnp.einsum('bqh,bqhk->bqk', iw.astype(jnp.float32), jax.nn.relu(dots)) * IDX_SCALE
    score = jnp.where(q_pos[:, None] >= k_pos[None, :], score, -jnp.inf)
    _, sel = lax.top_k(score, topk)
    valid = k_pos[sel] <= q_pos[None, :, None]
    take = jax.vmap(lambda rows, idx: rows[idx])
    k_sel, v_sel = take(k, sel), take(v, sel)
    s = jnp.einsum('bqhgd,bqkhd->bhgqk', q, k_sel).astype(jnp.float32) * (q.shape[-1] ** -0.5)
    s = jnp.where(valid[:, None, None], s, -jnp.inf)
    p = jax.nn.softmax(s, axis=-1).astype(v.dtype)
    return jnp.einsum('bhgqk,bqkhd->bqhgd', p, v_sel)


def fox_attend(q, cq, k, v, ck, mask):
    bias = jnp.moveaxis(cq, 1, -1)[..., :, None] - jnp.moveaxis(ck, 1, -1)[..., None, :]
    return softmax_attend(q, k, v, mask, bias)


def merge_heads(*outs):
    return jnp.concatenate([o.reshape(o.shape[0], o.shape[1], -1) for o in outs], axis=-1)


def over_query_blocks(attend, q_tree, q_pos):
    head = jax.tree_util.tree_map(lambda a: a[:, :N_META], q_tree)
    y_head = attend(head, q_pos[:N_META])
    n_blk = (q_pos.shape[0] - N_META) // Q_BLOCK

    def split(a):
        a = a[:, N_META:]
        a = a.reshape((a.shape[0], n_blk, Q_BLOCK) + a.shape[2:])
        return jnp.moveaxis(a, 1, 0)

    blocks = jax.tree_util.tree_map(split, q_tree)
    pos_blocks = q_pos[N_META:].reshape(n_blk, Q_BLOCK)
    y_blk = lax.map(lambda bp: attend(bp[0], bp[1]), (blocks, pos_blocks))
    y_blk = jnp.moveaxis(y_blk, 0, 1)
    y_blk = y_blk.reshape((y_blk.shape[0], -1) + y_blk.shape[3:])
    return jnp.concatenate([y_head, y_blk], axis=1)


def mla_project(p, pos, g_q_lat, w_uq, g_kv_lat, g_q):
    B, T = p.shape[:2]
    q_lat = rms_norm(p[..., :MLA_Q_LORA], g_q_lat)
    c_kv = rms_norm(p[..., MLA_Q_LORA:MLA_Q_LORA + MLA_KV_LORA], g_kv_lat)
    k_pe = rope(p[..., MLA_Q_LORA + MLA_KV_LORA:][:, :, None, :], pos)[:, :, 0]
    q = (q_lat @ w_uq).reshape(B, T, MLA_HEADS, MLA_QK)
    q = jnp.concatenate([q[..., :MLA_NOPE], rope(q[..., MLA_NOPE:], pos)], axis=-1)
    q = rms_norm(q, g_q)[:, :, :, None, :]
    return q, jnp.concatenate([c_kv, k_pe], axis=-1)


def mla_keys(lat, w_uk, w_uv, g_k):
    c_kv, k_pe = lat[..., :MLA_KV_LORA], lat[..., MLA_KV_LORA:]
    k_nope = jnp.einsum('...c,chd->...hd', c_kv, w_uk)
    k_pe = jnp.broadcast_to(k_pe[..., None, :], k_nope.shape[:-1] + (MLA_ROPE,))
    k = rms_norm(jnp.concatenate([k_nope, k_pe], axis=-1), g_k)
    v = jnp.einsum('...c,chd->...hd', c_kv, w_uv)
    return k, v


def dsa_project(p, pos, g_q, g_k, g_ik):
    B, T = p.shape[:2]
    q, k, v, iq, iw, ik = split_cols(p, DSA_SIZES)
    q = rope(rms_norm(q.reshape(B, T, DSA_HEADS, HEAD_DIM), g_q), pos)
    q = q.reshape(B, T, DSA_KV_HEADS, DSA_GROUP, HEAD_DIM)
    k = rope(rms_norm(k.reshape(B, T, DSA_KV_HEADS, HEAD_DIM), g_k), pos)
    v = v.reshape(B, T, DSA_KV_HEADS, HEAD_DIM)
    iq = rope(iq.reshape(B, T, IDX_HEADS, IDX_DIM), pos)
    ik = rope(rms_norm(ik, g_ik)[:, :, None, :], pos)[:, :, 0]
    return q, iq, iw, jnp.stack([k, v], axis=2), ik


def even_project(h, pos, prm):
    w_in, g_q_lat, w_uq, g_kv_lat, w_uk, w_uv, g_mq, g_mk, g_dq, g_dk, g_ik = prm
    p = h @ w_in
    q_mla, lat = mla_project(p[..., :MLA_IN], pos, g_q_lat, w_uq, g_kv_lat, g_mq)
    q_dsa, iq, iw, kv, ik = dsa_project(p[..., MLA_IN:], pos, g_dq, g_dk, g_ik)
    return q_mla, lat, q_dsa, iq, iw, kv, ik


def even_prompt(h, prm):
    L = h.shape[1]
    pos = jnp.arange(L)
    topk = min(TOPK_MAX, (L - N_META) // 4)
    q_mla, lat, q_dsa, iq, iw, kv, ik = even_project(h, pos, prm)
    k_mla, v_mla = mla_keys(lat, prm[4], prm[5], prm[7])
    k_d, v_d = kv[:, :, 0], kv[:, :, 1]

    def attend(tree, qp):
        qm, qd, iqb, iwb = tree
        om = softmax_attend(qm, k_mla, v_mla, qp[:, None] >= pos[None, :])
        od = dsa_attend(qd, iqb, iwb, k_d, v_d, ik, qp, pos, topk)
        return merge_heads(om, od)

    o = over_query_blocks(attend, (q_mla, q_dsa, iq, iw), pos)
    return o, (lat, kv, ik)


def even_sample(h, li, prm, page_table, cache_mla, cache_dsa_kv, cache_dsa_idx):
    T = h.shape[1]
    q_pos = PAST_LEN + jnp.arange(T)
    k_pos = jnp.arange(PAST_LEN + T)
    mask = q_pos[:, None] >= k_pos[None, :]
    topk = min(TOPK_MAX, (PAST_LEN + T) // 4)
    q_mla, lat, q_dsa, iq, iw, kv, ik = even_project(h, q_pos, prm)

    def one_seq(args):
        pages, qm, lat_n, qd, iqb, iwb, kv_n, ik_n = args
        lat_all = jnp.concatenate([gather_pages(cache_mla, li, pages), lat_n], axis=0)
        km, vm = mla_keys(lat_all, prm[4], prm[5], prm[7])
        om = softmax_attend(qm[None], km[None], vm[None], mask)
        kv_all = jnp.concatenate([gather_pages(cache_dsa_kv, li, pages), kv_n], axis=0)
        ik_all = jnp.concatenate([gather_pages(cache_dsa_idx, li, pages), ik_n], axis=0)
        od = dsa_attend(qd[None], iqb[None], iwb[None], kv_all[None, :, 0], kv_all[None, :, 1],
                        ik_all[None], q_pos, k_pos, topk)
        return merge_heads(om, od)[0]

    o = lax.map(one_seq, (page_table, q_mla, lat, q_dsa, iq, iw, kv, ik))
    return o, (lat, kv, ik)


def fox_project(h, prm):
    w_in, b_f, g_q, g_k = prm
    B, T = h.shape[:2]
    q, k, v, fl = split_cols(h @ w_in, FOX_SIZES)
    q = rms_norm(q.reshape(B, T, FOX_HEADS, HEAD_DIM), g_q).reshape(B, T, FOX_KV_HEADS, FOX_GROUP, HEAD_DIM)
    k = rms_norm(k.reshape(B, T, FOX_KV_HEADS, HEAD_DIM), g_k)
    v = v.reshape(B, T, FOX_KV_HEADS, HEAD_DIM)
    logf = jax.nn.log_sigmoid(fl.astype(jnp.float32) + b_f.astype(jnp.float32))
    return q, jnp.stack([k, v], axis=2), logf


def odd_prompt(h, prm):
    B, L = h.shape[:2]
    pos = jnp.arange(L)
    q, kv, logf = fox_project(h, prm)
    c = jnp.cumsum(logf, axis=1).reshape(B, L, FOX_KV_HEADS, FOX_GROUP)
    k, v = kv[:, :, 0], kv[:, :, 1]

    def attend(tree, qp):
        qb, cqb = tree
        return merge_heads(fox_attend(qb, cqb, k, v, c, qp[:, None] >= pos[None, :]))

    o = over_query_blocks(attend, (q, c), pos)
    return o, (kv, logf)


def odd_sample(h, li, prm, page_table, cache_fox_kv, cache_fox_logf):
    T = h.shape[1]
    q_pos = PAST_LEN + jnp.arange(T)
    k_pos = jnp.arange(PAST_LEN + T)
    mask = q_pos[:, None] >= k_pos[None, :]
    q, kv, logf = fox_project(h, prm)

    def one_seq(args):
        pages, qs, kv_n, lf_n = args
        kv_all = jnp.concatenate([gather_pages(cache_fox_kv, li, pages), kv_n], axis=0)
        lf_all = jnp.concatenate([gather_pages(cache_fox_logf, li, pages).astype(jnp.float32), lf_n], axis=0)
        c = jnp.cumsum(lf_all, axis=0).reshape(-1, FOX_KV_HEADS, FOX_GROUP)
        o = fox_attend(qs[None], c[None, -T:], kv_all[None, :, 0], kv_all[None, :, 1], c[None], mask)
        return merge_heads(o)[0]

    o = lax.map(one_seq, (page_table, q, kv, logf))
    return o, (kv, logf)


def swiglu(h, w_gate, w_up, w_down):
    return (jax.nn.silu(h @ w_gate) * (h @ w_up)) @ w_down


def setup_inputs(seed: int = 0) -> dict:
    key = jax.random.key(seed)
    ks = iter(jax.random.split(key, 40))
    n_pages = PAST_LEN // PAGE_SIZE
    n_used = DEC_BATCH * n_pages
    n_pool = n_used + max(1, n_used // 4)

    def nrm(shape, scale=1.0):
        return scale * jax.random.normal(next(ks), shape, jnp.float32)

    def gain(shape):
        return 1.0 + 0.05 * nrm(shape)

    page_table = jax.random.permutation(next(ks), n_pool)[:n_used].reshape(DEC_BATCH, n_pages).astype(jnp.int32)
    return {
        'x_prompt': nrm((BATCH, SEQ, D_MODEL)),
        'x_sample': nrm((DEC_BATCH, DEC_SEQ, D_MODEL)),
        'cache_mla': nrm((N_EVEN, n_pool, PAGE_SIZE, MLA_LAT)),
        'cache_dsa_kv': nrm((N_EVEN, n_pool, PAGE_SIZE, 2, DSA_KV_HEADS, HEAD_DIM)),
        'cache_dsa_idx': nrm((N_EVEN, n_pool, PAGE_SIZE, IDX_DIM)),
        'cache_fox_kv': nrm((N_ODD, n_pool, PAGE_SIZE, 2, FOX_KV_HEADS, HEAD_DIM)),
        'cache_fox_logf': jax.nn.log_sigmoid(2.5 + nrm((N_ODD, n_pool, PAGE_SIZE, FOX_HEADS))),
        'page_table': page_table,
        'meta_tokens': nrm((N_META, D_MODEL)),
        'g_mix': gain((DEPTH, D_MODEL)),
        'g_ffn': gain((DEPTH, D_MODEL)),
        'w_in_even': nrm((N_EVEN, D_MODEL, EVEN_IN), D_MODEL ** -0.5),
        'g_mla_q_lat': gain((N_EVEN, MLA_Q_LORA)),
        'w_mla_uq': nrm((N_EVEN, MLA_Q_LORA, MLA_HEADS * MLA_QK), MLA_Q_LORA ** -0.5),
        'g_mla_kv_lat': gain((N_EVEN, MLA_KV_LORA)),
        'w_mla_uk': nrm((N_EVEN, MLA_KV_LORA, MLA_HEADS, MLA_NOPE), MLA_KV_LORA ** -0.5),
        'w_mla_uv': nrm((N_EVEN, MLA_KV_LORA, MLA_HEADS, MLA_V), MLA_KV_LORA ** -0.5),
        'g_mla_q': gain((N_EVEN, MLA_QK)),
        'g_mla_k': gain((N_EVEN, MLA_QK)),
        'g_dsa_q': gain((N_EVEN, HEAD_DIM)),
        'g_dsa_k': gain((N_EVEN, HEAD_DIM)),
        'g_idx_k': gain((N_EVEN, IDX_DIM)),
        'w_out_even': nrm((N_EVEN, EVEN_MIX, D_MODEL), EVEN_MIX ** -0.5),
        'w_in_odd': nrm((N_ODD, D_MODEL, ODD_IN), D_MODEL ** -0.5),
        'b_fox_f': jax.random.uniform(next(ks), (N_ODD, FOX_HEADS), jnp.float32, 1.0, 4.0),
        'g_fox_q': gain((N_ODD, HEAD_DIM)),
        'g_fox_k': gain((N_ODD, HEAD_DIM)),
        'w_out_odd': nrm((N_ODD, ODD_MIX, D_MODEL), ODD_MIX ** -0.5),
        'w_ffn_gate': nrm((DEPTH, D_MODEL, D_FF), D_MODEL ** -0.5),
        'w_ffn_up': nrm((DEPTH, D_MODEL, D_FF), D_MODEL ** -0.5),
        'w_ffn_down': nrm((DEPTH, D_FF, D_MODEL), D_FF ** -0.5),
    }


def reference(x_prompt, x_sample, cache_mla, cache_dsa_kv, cache_dsa_idx, cache_fox_kv, cache_fox_logf,
              page_table, meta_tokens, g_mix, g_ffn, w_in_even, g_mla_q_lat, w_mla_uq, g_mla_kv_lat,
              w_mla_uk, w_mla_uv, g_mla_q, g_mla_k, g_dsa_q, g_dsa_k, g_idx_k, w_out_even, w_in_odd,
              b_fox_f, g_fox_q, g_fox_k, w_out_odd, w_ffn_gate, w_ffn_up, w_ffn_down):
    b = x_prompt.shape[0]
    meta = jnp.broadcast_to(meta_tokens.astype(x_prompt.dtype)[None], (b, N_META, D_MODEL))
    xp = jnp.concatenate([meta, x_prompt], axis=1)
    xs = x_sample
    mla_p, mla_s, dkv_p, dkv_s, idx_p, idx_s = [], [], [], [], [], []
    fkv_p, fkv_s, lf_p, lf_s = [], [], [], []
    for l in range(DEPTH):
        li = l // 2
        hp, hs = rms_norm(xp, g_mix[l]), rms_norm(xs, g_mix[l])
        if l % 2 == 0:
            prm = (w_in_even[li], g_mla_q_lat[li], w_mla_uq[li], g_mla_kv_lat[li], w_mla_uk[li], w_mla_uv[li],
                   g_mla_q[li], g_mla_k[li], g_dsa_q[li], g_dsa_k[li], g_idx_k[li])
            op, (lat_p, kv_p, ik_p) = even_prompt(hp, prm)
            osm, (lat_s, kv_s, ik_s) = even_sample(hs, li, prm, page_table, cache_mla, cache_dsa_kv, cache_dsa_idx)
            mla_p.append(lat_p); mla_s.append(lat_s)
            dkv_p.append(kv_p); dkv_s.append(kv_s)
            idx_p.append(ik_p); idx_s.append(ik_s)
            w_out = w_out_even[li]
        else:
            prm = (w_in_odd[li], b_fox_f[li], g_fox_q[li], g_fox_k[li])
            op, (kv_p, logf_p) = odd_prompt(hp, prm)
            osm, (kv_s, logf_s) = odd_sample(hs, li, prm, page_table, cache_fox_kv, cache_fox_logf)
            fkv_p.append(kv_p); fkv_s.append(kv_s)
            lf_p.append(logf_p); lf_s.append(logf_s)
            w_out = w_out_odd[li]
        xp = xp + op @ w_out
        xs = xs + osm @ w_out
        xp = xp + swiglu(rms_norm(xp, g_ffn[l]), w_ffn_gate[l], w_ffn_up[l], w_ffn_down[l])
        xs = xs + swiglu(rms_norm(xs, g_ffn[l]), w_ffn_gate[l], w_ffn_up[l], w_ffn_down[l])
    y_prompt = xp[:, N_META:]
    return (y_prompt, xs,
            jnp.stack(mla_p), jnp.stack(mla_s),
            jnp.stack(dkv_p), jnp.stack(dkv_s),
            jnp.stack(idx_p), jnp.stack(idx_s),
            jnp.stack(fkv_p), jnp.stack(fkv_s),
            jnp.stack(lf_p), jnp.stack(lf_s))
```

```python
import functools

import numpy as np
import jax
import jax.numpy as jnp
from jax import lax
from jax.experimental import pallas as pl
from jax.experimental.pallas import tpu as pltpu

F32 = jnp.float32
BF16 = jnp.bfloat16

N_META = 16
HEAD_DIM = 64
ROPE_THETA = 10000.0
NORM_EPS = 1e-6
MLA_HEADS = 8
MLA_NOPE = 64
MLA_ROPE = 32
MLA_QK = MLA_NOPE + MLA_ROPE
MLA_V = 64
MLA_Q_LORA = 384
MLA_KV_LORA = 256
MLA_LAT = MLA_KV_LORA + MLA_ROPE
DSA_HEADS = 8
DSA_KV_HEADS = 2
DSA_GROUP = DSA_HEADS // DSA_KV_HEADS
IDX_HEADS = 8
IDX_DIM = 64
TOPK_MAX = 256
IDX_SCALE = (IDX_DIM * IDX_HEADS) ** -0.5
FOX_HEADS = 16
FOX_KV_HEADS = 8

LANES = 128
VMEM_LIMIT = 56 * 1024 * 1024
NEG = -1e30
INT_MIN = -(2 ** 31)

P_QL, P_CKV, P_DQ, P_DK, P_DV, P_IQ, P_MISC, P_END = 0, 384, 640, 1152, 1280, 1408, 1920, 2048
M_KPE, M_IW = 64, 96


def _pick_tile(n):
    for t in (384, 256, 128):
        if n % t == 0:
            return t
    raise ValueError(f"row count {n} is not a multiple of 128")


def _cparams(sem):
    return pltpu.CompilerParams(dimension_semantics=sem, vmem_limit_bytes=VMEM_LIMIT)


def _lane(shape):
    return lax.broadcasted_iota(jnp.int32, shape, len(shape) - 1)


def _partner(x, d):
    lane = _lane(x.shape)
    return jnp.where((lane & d) != 0, pltpu.roll(x, d, 1), pltpu.roll(x, LANES - d, 1))


def _seg_allsum(x, seg):
    d = seg // 2
    while d >= 1:
        x = x + _partner(x, d)
        d //= 2
    return x


def _swap_halves(x):
    return pltpu.roll(x, 64, 1)


def _rms_rows(x, g):
    ss = jnp.sum(x * x, axis=-1, keepdims=True)
    return x * lax.rsqrt(ss / x.shape[-1] + NORM_EPS) * g


def _dot(a, b):
    return jnp.dot(a, b, preferred_element_type=F32)


def _dot_nt(a, b):
    return lax.dot_general(a, b, (((1,), (1,)), ((), ())), preferred_element_type=F32)


def _even_proj_kernel(x_ref, tab_ref, gmix_ref, win_ref, gql_ref, wuq_ref, gkv_ref, wuk_ref, wuv_ref,
                      gq_ref, gdq_ref, gdk_ref, gmisc_ref,
                      qm_ref, km_ref, vm_ref, lat_ref, qd_ref, kvd_ref, kdb_ref, iq_ref, misc_ref):
    x = x_ref[0]
    tm = x.shape[0]
    h = _rms_rows(x, gmix_ref[...]).astype(BF16)
    p = _dot(h, win_ref[...])
    lane = _lane((tm, LANES))
    lo = lane < 64
    cos64, sin64 = tab_ref[:, 0:128], tab_ref[:, 128:256]
    cosq, sinq = tab_ref[:, 256:384], tab_ref[:, 384:512]
    cosm, sinm = tab_ref[:, 512:640], tab_ref[:, 640:768]

    ql = _rms_rows(p[:, P_QL:P_CKV], gql_ref[...]).astype(BF16)
    qm = _dot(ql, wuq_ref[...])
    gq = gq_ref[...]
    for hh in range(MLA_HEADS):
        c = qm[:, LANES * hh:LANES * (hh + 1)]
        c = c * cosq + _partner(c, 16) * sinq
        ss = jnp.sum(c * c, axis=-1, keepdims=True)
        c = c * lax.rsqrt(ss / MLA_QK + NORM_EPS) * gq
        qm_ref[0, :, LANES * hh:LANES * (hh + 1)] = c.astype(BF16)

    ckv = _rms_rows(p[:, P_CKV:P_DQ], gkv_ref[...])
    lat_ref[0, :, 0:MLA_KV_LORA] = ckv

    m = p[:, P_MISC:P_END]
    ikss = jnp.sum(jnp.where(lo, m * m, 0.0), axis=-1, keepdims=True)
    m = m * jnp.where(lo, lax.rsqrt(ikss / IDX_DIM + NORM_EPS) * gmisc_ref[...], 1.0)
    rot = jnp.where(lo, _partner(m, 32), _partner(m, 16))
    m = m * cosm + rot * sinm
    misc_ref[0] = m
    lat_ref[0, :, MLA_KV_LORA:MLA_LAT] = m[:, M_KPE:M_KPE + MLA_ROPE]

    ckv_b = ckv.astype(BF16)
    kn = _dot(ckv_b, wuk_ref[...])
    kpe = jnp.where((lane >= M_KPE) & (lane < M_KPE + MLA_ROPE), m, 0.0)
    for hh in range(MLA_HEADS):
        c = kn[:, LANES * hh:LANES * (hh + 1)] + kpe
        ss = jnp.sum(c * c, axis=-1, keepdims=True)
        km_ref[0, :, LANES * hh:LANES * (hh + 1)] = (c * lax.rsqrt(ss / MLA_QK + NORM_EPS)).astype(BF16)
    vm_ref[0] = _dot(ckv_b, wuv_ref[...]).astype(BF16)

    gdq = gdq_ref[...]
    for j in range(DSA_HEADS // 2):
        c = p[:, P_DQ + LANES * j:P_DQ + LANES * (j + 1)]
        c = c * lax.rsqrt(_seg_allsum(c * c, 64) / HEAD_DIM + NORM_EPS) * gdq
        c = c * cos64 + _partner(c, 32) * sin64
        sw = _swap_halves(c)
        if (2 * j) // DSA_GROUP == 0:
            a, b = jnp.where(lo, c, 0.0), jnp.where(lo, sw, 0.0)
        else:
            a, b = jnp.where(lo, 0.0, sw), jnp.where(lo, 0.0, c)
        qd_ref[0, :, LANES * (2 * j):LANES * (2 * j + 1)] = a.astype(BF16)
        qd_ref[0, :, LANES * (2 * j + 1):LANES * (2 * j + 2)] = b.astype(BF16)

    c = p[:, P_DK:P_DV]
    c = c * lax.rsqrt(_seg_allsum(c * c, 64) / HEAD_DIM + NORM_EPS) * gdk_ref[...]
    kd = c * cos64 + _partner(c, 32) * sin64
    vd = p[:, P_DV:P_IQ]
    kvd_ref[0, :, 0:128] = kd
    kvd_ref[0, :, 128:256] = vd
    kdb_ref[0, :, 0:128] = kd.astype(BF16)
    kdb_ref[0, :, 128:256] = vd.astype(BF16)
    kdb_ref[0, :, 256:384] = m.astype(BF16)

    for j in range(IDX_HEADS // 2):
        c = p[:, P_IQ + LANES * j:P_IQ + LANES * (j + 1)]
        c = c * cos64 + _partner(c, 32) * sin64
        iq_ref[0, :, LANES * (2 * j):LANES * (2 * j + 1)] = jnp.where(lo, c, 0.0).astype(BF16)
        iq_ref[0, :, LANES * (2 * j + 1):LANES * (2 * j + 2)] = jnp.where(lo, _swap_halves(c), 0.0).astype(BF16)


def _rope_tables(pos):
    pos = pos.astype(F32)[:, None]

    def cs(half):
        inv = ROPE_THETA ** (-jnp.arange(half, dtype=F32) / half)
        ang = pos * inv
        c, s = jnp.cos(ang), jnp.sin(ang)
        return jnp.concatenate([c, c], -1), jnp.concatenate([-s, s], -1)

    c64, s64 = cs(32)
    c32, s32 = cs(16)
    t = pos.shape[0]
    one, zero = jnp.ones((t, 32), F32), jnp.zeros((t, 32), F32)
    cos64, sin64 = jnp.concatenate([c64, c64], -1), jnp.concatenate([s64, s64], -1)
    cosq = jnp.concatenate([one, one, c32, one], -1)
    sinq = jnp.concatenate([zero, zero, s32, zero], -1)
    cosm = jnp.concatenate([c64, c32, one], -1)
    sinm = jnp.concatenate([s64, s32, zero], -1)
    return jnp.concatenate([cos64, sin64, cosq, sinq, cosm, sinm], -1)


def _even_weights(w_in, g_q_lat, w_uq, g_kv_lat, w_uk, w_uv, g_mq, g_mk, g_dq, g_dk, g_ik):
    d = w_in.shape[0]
    o_dsa = MLA_Q_LORA + MLA_KV_LORA + MLA_ROPE
    o_dq, o_dk, o_dv, o_iq = o_dsa, o_dsa + 512, o_dsa + 640, o_dsa + 768
    o_iw, o_ik = o_iq + 512, o_iq + 520
    win = jnp.concatenate([
        w_in[:, 0:640], w_in[:, o_dq:o_iw], w_in[:, o_ik:o_ik + 64], w_in[:, 640:672], w_in[:, o_iw:o_iw + 8],
        jnp.zeros((d, 24), w_in.dtype)], axis=1).astype(BF16)
    wuq = w_uq.reshape(MLA_Q_LORA, MLA_HEADS, MLA_QK)
    wuq = jnp.pad(wuq, ((0, 0), (0, 0), (0, LANES - MLA_QK))).reshape(MLA_Q_LORA, MLA_HEADS * LANES).astype(BF16)
    wuk = jnp.pad(w_uk, ((0, 0), (0, 0), (0, LANES - MLA_NOPE))).reshape(MLA_KV_LORA, MLA_HEADS * LANES).astype(BF16)
    wuv = w_uv.reshape(MLA_KV_LORA, MLA_HEADS * MLA_V).astype(BF16)
    gq = jnp.concatenate([g_mq * g_mk * (MLA_QK ** -0.5), jnp.zeros((LANES - MLA_QK,), F32)])[None]
    gdq = (jnp.concatenate([g_dq, g_dq]) * (HEAD_DIM ** -0.5))[None]
    gdk = jnp.concatenate([g_dk, g_dk])[None]
    gmisc = jnp.concatenate([g_ik, jnp.ones((64,), F32)])[None]
    return win, g_q_lat[None], wuq, g_kv_lat[None], wuk, wuv, gq, gdq, gdk, gmisc


def _even_project(x, t_out, tab, g_mix, ew):
    b, tp, d = x.shape
    tm = _pick_tile(tp)
    win, gql, wuq, gkv, wuk, wuv, gq, gdq, gdk, gmisc = ew
    row = lambda w: pl.BlockSpec((1, tm, w), lambda i, j: (i, j, 0))
    full = lambda a: pl.BlockSpec(a.shape, lambda i, j: (0,) * a.ndim)
    outs = [
        jax.ShapeDtypeStruct((b, tp, 1024), BF16), jax.ShapeDtypeStruct((b, tp, 1024), BF16),
        jax.ShapeDtypeStruct((b, tp, 512), BF16), jax.ShapeDtypeStruct((b, t_out, MLA_LAT), F32),
        jax.ShapeDtypeStruct((b, tp, 1024), BF16), jax.ShapeDtypeStruct((b, t_out, 256), F32),
        jax.ShapeDtypeStruct((b, tp, 384), BF16), jax.ShapeDtypeStruct((b, tp, 1024), BF16),
        jax.ShapeDtypeStruct((b, tp, LANES), F32),
    ]
    params = (g_mix[None], win, gql, wuq, gkv, wuk, wuv, gq, gdq, gdk, gmisc)
    return pl.pallas_call(
        _even_proj_kernel,
        grid=(b, tp // tm),
        in_specs=[row(d), pl.BlockSpec((tm, 768), lambda i, j: (j, 0))] + [full(a) for a in params],
        out_specs=[row(1024), row(1024), row(512), row(MLA_LAT), row(1024), row(256), row(384), row(1024), row(LANES)],
        out_shape=outs,
        compiler_params=_cparams(("parallel", "parallel")),
        name="even_project",
    )(x, tab, *params)


def _softmax_step(s, v, m_ref, l_ref, acc_ref):
    m_prev = m_ref[...]
    m_new = jnp.maximum(m_prev, jnp.max(s, axis=-1, keepdims=True))
    alpha = jnp.exp(m_prev - m_new)
    p = jnp.exp(s - m_new)
    l_ref[...] = alpha * l_ref[...] + jnp.sum(p, axis=-1, keepdims=True)
    acc_ref[...] = alpha * acc_ref[...] + _dot(p.astype(BF16), v)
    m_ref[...] = m_new


def _causal_tile_mask(t):
    return lax.broadcasted_iota(jnp.int32, (t, t), 0) >= lax.broadcasted_iota(jnp.int32, (t, t), 1)


def _mla_prompt_kernel(q_ref, k_ref, v_ref, o_ref, m_ref, l_ref, acc_ref, *, t):
    qi = pl.program_id(1)
    diag = _causal_tile_mask(t)
    lo = _lane((t, LANES)) < 64
    for pair in range(MLA_HEADS // 2):
        outs = []
        for hh in (2 * pair, 2 * pair + 1):
            q = q_ref[0, :, LANES * hh:LANES * (hh + 1)]
            m_ref[...] = jnp.full(m_ref.shape, NEG, F32)
            l_ref[...] = jnp.zeros(l_ref.shape, F32)
            acc_ref[...] = jnp.zeros(acc_ref.shape, F32)

            def step(j, masked, q=q, hh=hh):
                rows = pl.ds(pl.multiple_of(j * t, t), t)
                s = _dot_nt(q, k_ref[0, rows, LANES * hh:LANES * (hh + 1)])
                if masked:
                    s = jnp.where(diag, s, NEG)
                _softmax_step(s, v_ref[0, rows, LANES * pair:LANES * (pair + 1)], m_ref, l_ref, acc_ref)

            lax.fori_loop(0, qi, lambda j, c: (step(j, False), c)[1], 0)
            step(qi, True)
            outs.append(acc_ref[...] / l_ref[...])
        o_ref[0, :, LANES * pair:LANES * (pair + 1)] = jnp.where(lo, outs[0], outs[1]).astype(o_ref.dtype)


def _mla_prompt(qm, km, vm):
    b, tp, _ = qm.shape
    t = _pick_tile(tp)
    return pl.pallas_call(
        functools.partial(_mla_prompt_kernel, t=t),
        grid=(b, tp // t),
        in_specs=[pl.BlockSpec((1, t, 1024), lambda i, j: (i, j, 0)),
                  pl.BlockSpec((1, tp, 1024), lambda i, j: (i, 0, 0)),
                  pl.BlockSpec((1, tp, 512), lambda i, j: (i, 0, 0))],
        out_specs=pl.BlockSpec((1, t, 512), lambda i, j: (i, j, 0)),
        out_shape=jax.ShapeDtypeStruct((b, tp, 512), BF16),
        scratch_shapes=[pltpu.VMEM((t, 1), F32), pltpu.VMEM((t, 1), F32), pltpu.VMEM((t, LANES), F32)],
        compiler_params=_cparams(("parallel", "arbitrary")),
        name="mla_prompt_attention",
    )(qm, km, vm)


def _sort_key(score):
    bits = pltpu.bitcast(score + 0.0, jnp.int32)
    return jnp.where(bits < 0, bits ^ 0x7FFFFFFF, bits)


def _row_count(pred):
    return jnp.sum(pred.astype(F32), axis=-1, keepdims=True)


def _topk_threshold(count, rows, k, idx_bits):
    kf = jnp.float32(k)
    t0 = jnp.where(count(lambda key, idx: key >= 0) >= kf, 0, INT_MIN).astype(jnp.int32)

    def key_bit(i, t):
        cand = t | (jnp.int32(1) << (30 - i))
        return jnp.where(count(lambda key, idx: key >= cand) >= kf, cand, t)

    thr = lax.fori_loop(0, 31, key_bit, t0)
    n_ge = count(lambda key, idx: key >= thr)
    all_idx = jnp.full((rows, 1), (1 << idx_bits) - 1, jnp.int32)

    def tie_bound():
        need = kf - count(lambda key, idx: key > thr)

        def idx_bit(i, hi):
            cand = hi & ~(jnp.int32(1) << (idx_bits - 1 - i))
            ok = count(lambda key, idx: (key == thr) & (idx <= cand)) >= need
            return jnp.where(ok, cand, hi)

        return lax.fori_loop(0, idx_bits, idx_bit, all_idx)

    has_tie = jnp.max(jnp.where((n_ge > kf) & (thr > INT_MIN), 1.0, 0.0)) > 0.0
    return thr, lax.cond(has_tie, tie_bound, lambda: all_idx)


def _selected(key, idx, thr, bound):
    return ((key > thr) | ((key == thr) & (idx <= bound))) & (key > INT_MIN)


def _dsa_prompt_kernel(q_ref, iq_ref, misc_ref, kdb_ref, o_ref, key_ref, bias_ref, m_ref, l_ref, acc_ref,
                       *, t, topk, idx_bits):
    qi = pl.program_id(1)
    nk = qi + 1
    row_pos = qi * t + lax.broadcasted_iota(jnp.int32, (t, t), 0)
    col_iota = lax.broadcasted_iota(jnp.int32, (t, t), 1)
    iw = misc_ref[0][:, M_IW:M_IW + IDX_HEADS] * IDX_SCALE

    def score_chunk(c, carry):
        rows = pl.ds(pl.multiple_of(c * t, t), t)
        ik = kdb_ref[0, rows, 256:384]
        acc = jnp.zeros((t, t), F32)
        for hh in range(IDX_HEADS):
            d = _dot_nt(iq_ref[0, :, LANES * hh:LANES * (hh + 1)], ik)
            acc = acc + iw[:, hh:hh + 1] * jnp.maximum(d, 0.0)
        ok = row_pos >= c * t + col_iota
        key_ref[c] = jnp.where(ok, _sort_key(acc), INT_MIN)
        return carry

    lax.fori_loop(0, nk, score_chunk, 0)

    def count(f):
        def body(c, acc):
            return acc + _row_count(f(key_ref[c], c * t + col_iota))
        return lax.fori_loop(0, nk, body, jnp.zeros((t, 1), F32))

    thr, bound = _topk_threshold(count, t, topk, idx_bits)

    m_ref[...] = jnp.full(m_ref.shape, NEG, F32)
    l_ref[...] = jnp.zeros(l_ref.shape, F32)
    acc_ref[...] = jnp.zeros(acc_ref.shape, F32)

    def attend_chunk(c, carry):
        rows = pl.ds(pl.multiple_of(c * t, t), t)
        key = key_ref[c]
        bias_ref[...] = jnp.where(_selected(key, c * t + col_iota, thr, bound), 0.0, NEG)
        kc = kdb_ref[0, rows, 0:128]
        vc = kdb_ref[0, rows, 128:256]
        for hh in range(DSA_HEADS):
            s = _dot_nt(q_ref[0, :, LANES * hh:LANES * (hh + 1)], kc) + bias_ref[...]
            _softmax_step(s, vc, m_ref.at[hh], l_ref.at[hh], acc_ref.at[hh])
        return carry

    lax.fori_loop(0, nk, attend_chunk, 0)

    lo = _lane((t, LANES)) < 64
    for pair in range(DSA_HEADS // 2):
        a = acc_ref[2 * pair] / l_ref[2 * pair]
        b = acc_ref[2 * pair + 1] / l_ref[2 * pair + 1]
        if (2 * pair) // DSA_GROUP == 0:
            out = jnp.where(lo, a, _swap_halves(b))
        else:
            out = jnp.where(lo, _swap_halves(a), b)
        o_ref[0, :, LANES * pair:LANES * (pair + 1)] = out.astype(o_ref.dtype)


def _dsa_prompt(qd, iq, misc, kdb, topk):
    b, tp, _ = qd.shape
    t = _pick_tile(tp)
    n = tp // t
    idx_bits = max(1, int(tp - 1).bit_length())
    return pl.pallas_call(
        functools.partial(_dsa_prompt_kernel, t=t, topk=topk, idx_bits=idx_bits),
        grid=(b, n),
        in_specs=[pl.BlockSpec((1, t, 1024), lambda i, j: (i, j, 0)),
                  pl.BlockSpec((1, t, 1024), lambda i, j: (i, j, 0)),
                  pl.BlockSpec((1, t, LANES), lambda i, j: (i, j, 0)),
                  pl.BlockSpec((1, tp, 384), lambda i, j: (i, 0, 0))],
        out_specs=pl.BlockSpec((1, t, 512), lambda i, j: (i, j, 0)),
        out_shape=jax.ShapeDtypeStruct((b, tp, 512), BF16),
        scratch_shapes=[pltpu.VMEM((n, t, t), jnp.int32), pltpu.VMEM((t, t), F32),
                        pltpu.VMEM((DSA_HEADS, t, 1), F32), pltpu.VMEM((DSA_HEADS, t, 1), F32),
                        pltpu.VMEM((DSA_HEADS, t, LANES), F32)],
        compiler_params=_cparams(("parallel", "arbitrary")),
        name="dsa_prompt_attention",
    )(qd, iq, misc, kdb)


def _mix_ffn_kernel(*refs, n_mix):
    x_ref = refs[0]
    o_refs = refs[1:1 + n_mix]
    w_refs = refs[1 + n_mix:1 + 2 * n_mix]
    g_ref, wg_ref, wu_ref, wd_ref, y_ref = refs[1 + 2 * n_mix:]
    x = x_ref[0]
    for o_ref, w_ref in zip(o_refs, w_refs):
        x = x + _dot(o_ref[0], w_ref[...])
    h = _rms_rows(x, g_ref[...]).astype(BF16)
    gate = _dot(h, wg_ref[...])
    up = _dot(h, wu_ref[...])
    act = (gate * jax.nn.sigmoid(gate) * up).astype(BF16)
    y_ref[0] = x + _dot(act, wd_ref[...])


def _mix_ffn(x, mixes, w_outs, g, wg, wu, wd):
    b, tp, d = x.shape
    tm = _pick_tile(tp)
    n_mix = len(mixes)
    row = lambda w: pl.BlockSpec((1, tm, w), lambda i, j: (i, j, 0))
    full = lambda a: pl.BlockSpec(a.shape, lambda i, j: (0,) * a.ndim)
    params = (g[None], wg, wu, wd)
    return pl.pallas_call(
        functools.partial(_mix_ffn_kernel, n_mix=n_mix),
        grid=(b, tp // tm),
        in_specs=[row(d)] + [row(o.shape[-1]) for o in mixes] + [full(w) for w in w_outs] + [full(a) for a in params],
        out_specs=row(d),
        out_shape=jax.ShapeDtypeStruct((b, tp, d), F32),
        compiler_params=_cparams(("parallel", "parallel")),
        name="mix_ffn",
    )(x, *mixes, *w_outs, *params)


O_Q, O_K, O_V, O_F, O_END = 0, 1024, 1536, 2048, 2176


def _split3(x):
    a = x.astype(BF16)
    r = x - a.astype(F32)
    b = r.astype(BF16)
    c = (r - b.astype(F32)).astype(BF16)
    return a, b, c


def _odd_proj_kernel(x_ref, gmix_ref, win_ref, bf_ref, gq_ref, gk_ref,
                     q_ref, kv_ref, kvb_ref, lf_ref, c_ref, ct_ref, carry_ref):
    @pl.when(pl.program_id(1) == 0)
    def _():
        carry_ref[...] = jnp.zeros(carry_ref.shape, F32)

    x = x_ref[0]
    tm = x.shape[0]
    h = _rms_rows(x, gmix_ref[...]).astype(BF16)
    p = _dot(h, win_ref[...])
    lo = _lane((tm, LANES)) < 64
    gq, gk = gq_ref[...], gk_ref[...]
    for j in range(FOX_HEADS // 2):
        c = p[:, O_Q + LANES * j:O_Q + LANES * (j + 1)]
        c = c * lax.rsqrt(_seg_allsum(c * c, 64) / HEAD_DIM + NORM_EPS) * gq
        sw = _swap_halves(c)
        if j % 2 == 0:
            a, b = jnp.where(lo, c, 0.0), jnp.where(lo, sw, 0.0)
        else:
            a, b = jnp.where(lo, 0.0, sw), jnp.where(lo, 0.0, c)
        q_ref[0, :, LANES * (2 * j):LANES * (2 * j + 1)] = a.astype(BF16)
        q_ref[0, :, LANES * (2 * j + 1):LANES * (2 * j + 2)] = b.astype(BF16)
    for j in range(FOX_KV_HEADS // 2):
        c = p[:, O_K + LANES * j:O_K + LANES * (j + 1)]
        c = c * lax.rsqrt(_seg_allsum(c * c, 64) / HEAD_DIM + NORM_EPS) * gk
        kv_ref[0, :, LANES * j:LANES * (j + 1)] = c
        kvb_ref[0, :, LANES * j:LANES * (j + 1)] = c.astype(BF16)
    v = p[:, O_V:O_F]
    kv_ref[0, :, 512:1024] = v
    kvb_ref[0, :, 512:1024] = v.astype(BF16)

    lf = jax.nn.log_sigmoid(p[:, O_F:O_END] + bf_ref[...])
    lf_ref[0] = lf[:, 0:FOX_HEADS]
    tri = (lax.broadcasted_iota(jnp.int32, (tm, tm), 0) >= lax.broadcasted_iota(jnp.int32, (tm, tm), 1)).astype(BF16)
    cum = carry_ref[0:1, :]
    for part in _split3(lf):
        cum = cum + _dot(tri, part)
    c_ref[0] = cum
    ct_ref[0, 0] = cum.T[0:FOX_HEADS, :]
    carry_ref[0:1, :] = cum[tm - 1:tm, :]


def _odd_weights(w_in, b_f, g_q, g_k):
    d = w_in.shape[0]
    win = jnp.concatenate([w_in, jnp.zeros((d, O_END - w_in.shape[1]), w_in.dtype)], axis=1).astype(BF16)
    bf = jnp.concatenate([b_f, jnp.zeros((LANES - FOX_HEADS,), F32)])[None]
    gq = (jnp.concatenate([g_q, g_q]) * (HEAD_DIM ** -0.5))[None]
    gk = jnp.concatenate([g_k, g_k])[None]
    return win, bf, gq, gk


def _odd_project(x, t_out, g_mix, ow):
    b, tp, d = x.shape
    tm = _pick_tile(tp)
    n = tp // tm
    win, bf, gq, gk = ow
    row = lambda w: pl.BlockSpec((1, tm, w), lambda i, j: (i, j, 0))
    full = lambda a: pl.BlockSpec(a.shape, lambda i, j: (0,) * a.ndim)
    params = (g_mix[None], win, bf, gq, gk)
    outs = [
        jax.ShapeDtypeStruct((b, tp, 2048), BF16), jax.ShapeDtypeStruct((b, t_out, 1024), F32),
        jax.ShapeDtypeStruct((b, tp, 1024), BF16), jax.ShapeDtypeStruct((b, t_out, FOX_HEADS), F32),
        jax.ShapeDtypeStruct((b, tp, LANES), F32), jax.ShapeDtypeStruct((b, n, FOX_HEADS, tm), F32),
    ]
    return pl.pallas_call(
        _odd_proj_kernel,
        grid=(b, n),
        in_specs=[row(d)] + [full(a) for a in params],
        out_specs=[row(2048), row(1024), row(1024), row(FOX_HEADS), row(LANES),
                   pl.BlockSpec((1, 1, FOX_HEADS, tm), lambda i, j: (i, j, 0, 0))],
        out_shape=outs,
        scratch_shapes=[pltpu.VMEM((8, LANES), F32)],
        compiler_params=_cparams(("parallel", "arbitrary")),
        name="odd_project",
    )(x, *params)


def _fox_prompt_kernel(q_ref, kv_ref, c_ref, ct_ref, o_ref, m_ref, l_ref, acc_ref, *, t):
    qi = pl.program_id(1)
    diag = _causal_tile_mask(t)
    lo = _lane((t, LANES)) < 64
    cq_all = c_ref[0]
    for kvh in range(FOX_KV_HEADS):
        grp, half = kvh // 2, kvh % 2
        outs = []
        for hh in (2 * kvh, 2 * kvh + 1):
            q = q_ref[0, :, LANES * hh:LANES * (hh + 1)]
            cq = cq_all[:, hh:hh + 1]
            m_ref[...] = jnp.full(m_ref.shape, NEG, F32)
            l_ref[...] = jnp.zeros(l_ref.shape, F32)
            acc_ref[...] = jnp.zeros(acc_ref.shape, F32)

            def step(j, masked, q=q, cq=cq, hh=hh, grp=grp):
                rows = pl.ds(pl.multiple_of(j * t, t), t)
                s = _dot_nt(q, kv_ref[0, rows, LANES * grp:LANES * (grp + 1)])
                s = s + (cq - ct_ref[0, j, hh:hh + 1, :])
                if masked:
                    s = jnp.where(diag, s, NEG)
                _softmax_step(s, kv_ref[0, rows, 512 + LANES * grp:512 + LANES * (grp + 1)], m_ref, l_ref, acc_ref)

            lax.fori_loop(0, qi, lambda j, c: (step(j, False), c)[1], 0)
            step(qi, True)
            outs.append(acc_ref[...] / l_ref[...])
        a, b = outs
        if half == 0:
            out = jnp.where(lo, a, _swap_halves(b))
        else:
            out = jnp.where(lo, _swap_halves(a), b)
        o_ref[0, :, LANES * kvh:LANES * (kvh + 1)] = out.astype(o_ref.dtype)


def _fox_prompt(q, kvb, c, ct):
    b, tp, _ = q.shape
    t = _pick_tile(tp)
    n = tp // t
    return pl.pallas_call(
        functools.partial(_fox_prompt_kernel, t=t),
        grid=(b, n),
        in_specs=[pl.BlockSpec((1, t, 2048), lambda i, j: (i, j, 0)),
                  pl.BlockSpec((1, tp, 1024), lambda i, j: (i, 0, 0)),
                  pl.BlockSpec((1, t, LANES), lambda i, j: (i, j, 0)),
                  pl.BlockSpec((1, n, FOX_HEADS, t), lambda i, j: (i, 0, 0, 0))],
        out_specs=pl.BlockSpec((1, t, 1024), lambda i, j: (i, j, 0)),
        out_shape=jax.ShapeDtypeStruct((b, tp, 1024), BF16),
        scratch_shapes=[pltpu.VMEM((t, 1), F32), pltpu.VMEM((t, 1), F32), pltpu.VMEM((t, LANES), F32)],
        compiler_params=_cparams(("parallel", "arbitrary")),
        name="fox_prompt_attention",
    )(q, kvb, c, ct)


def _page_dma(cache_ref, li, pt_ref, seq, page0, n_sub, ppc, buf_ref, slot, sem, wait):
    width = cache_ref.shape[-1]

    def body(sub, carry):
        for pp in range(ppc):
            page = pt_ref[seq, page0 + sub * ppc + pp]
            cp = pltpu.make_async_copy(cache_ref.at[li, page],
                                       buf_ref.at[slot, sub, :, pl.ds(pp * width, width)], sem.at[slot])
            if wait:
                cp.wait()
            else:
                cp.start()
        return carry

    lax.fori_loop(0, n_sub, body, 0)


def _softmax_step_nt(s, vt, m_ref, l_ref, acc_ref):
    m_prev = m_ref[...]
    m_new = jnp.maximum(m_prev, jnp.max(s, axis=-1, keepdims=True))
    alpha = jnp.exp(m_prev - m_new)
    p = jnp.exp(s - m_new)
    l_ref[...] = alpha * l_ref[...] + jnp.sum(p, axis=-1, keepdims=True)
    acc_ref[...] = alpha * acc_ref[...] + _dot_nt(p.astype(BF16), vt)
    m_ref[...] = m_new


def _init_softmax(m_ref, l_ref, acc_ref):
    m_ref[...] = jnp.full(m_ref.shape, NEG, F32)
    l_ref[...] = jnp.zeros(l_ref.shape, F32)
    acc_ref[...] = jnp.zeros(acc_ref.shape, F32)


def _tail_mask(rows, heads):
    r = lax.broadcasted_iota(jnp.int32, (rows, LANES), 0)
    return _lane((rows, LANES)) <= r // heads


def _feature_major_tail(x, n_seq):
    t_new = x.shape[0] // n_seq
    xt = jnp.swapaxes(x.reshape(n_seq, t_new, x.shape[1]), 1, 2)
    return jnp.pad(xt, ((0, 0), (0, 0), (0, LANES - t_new)))


def _mla_sample_kernel(pt_ref, cache_ref, qn_ref, qr_ref, tail_ref, wukt_ref, wuv_ref, o_ref,
                       buf, sem, m_ref, l_ref, acc_ref, *, li, n_sub, ppc, t_new):
    s = pl.program_id(0)
    slot = s % 2
    dma = functools.partial(_page_dma, cache_ref, li, pt_ref, page0=0, n_sub=n_sub, ppc=ppc, buf_ref=buf, sem=sem)

    @pl.when(s == 0)
    def _():
        dma(seq=s, slot=slot, wait=False)

    @pl.when(s + 1 < pl.num_programs(0))
    def _():
        dma(seq=s + 1, slot=1 - slot, wait=False)

    dma(seq=s, slot=slot, wait=True)

    rows = t_new * MLA_HEADS
    wukt = wukt_ref[...]
    qabs = _dot(qn_ref[0], wukt).astype(BF16)
    qr = qr_ref[0]
    _init_softmax(m_ref, l_ref, acc_ref)

    def block(lat_t, mask):
        ckv = lat_t[0:MLA_KV_LORA].astype(BF16)
        kpe = lat_t[MLA_KV_LORA:MLA_LAT]
        kn = _dot(wukt, ckv)
        tk = kn.shape[-1]
        ss = jnp.sum((kn * kn).reshape(MLA_HEADS, MLA_NOPE, tk), axis=1)
        ss = ss + jnp.sum(kpe * kpe, axis=0, keepdims=True)
        rs = lax.rsqrt(ss / MLA_QK + NORM_EPS)
        sc = _dot(qabs, ckv) + _dot(qr, kpe.astype(BF16))
        sc = sc * jnp.concatenate([rs] * t_new, axis=0)
        if mask is not None:
            sc = jnp.where(mask, sc, NEG)
        _softmax_step_nt(sc, ckv, m_ref, l_ref, acc_ref)

    def main(sub, carry):
        block(buf[slot, sub], None)
        return carry

    lax.fori_loop(0, n_sub, main, 0)
    block(tail_ref[0], _tail_mask(rows, MLA_HEADS))

    o_lat = (acc_ref[...] / l_ref[...]).astype(BF16)
    o = _dot(o_lat, wuv_ref[...])
    own = lax.broadcasted_iota(jnp.int32, (MLA_HEADS, 512), 1) // MLA_V == \
        lax.broadcasted_iota(jnp.int32, (MLA_HEADS, 512), 0)
    for q in range(t_new):
        blk = jnp.where(own, o[MLA_HEADS * q:MLA_HEADS * (q + 1)], 0.0)
        o_ref[0, q:q + 1, :] = jnp.sum(blk, axis=0, keepdims=True).astype(o_ref.dtype)


def _sub_pages(n_pages, want):
    ppc = want
    while n_pages % ppc:
        ppc //= 2
    return ppc


def _mla_sample(page_table, cache_t, li, qm, lat_new, wukt, wuv, n_seq):
    t_new = qm.shape[0] // n_seq
    rows = t_new * MLA_HEADS
    n_pages = page_table.shape[1]
    page = cache_t.shape[-1]
    ppc = _sub_pages(n_pages, 8)
    n_sub = n_pages // ppc
    q4 = qm.reshape(n_seq, t_new, MLA_HEADS, LANES)
    eye = jnp.eye(MLA_HEADS, dtype=qm.dtype)
    qn = (q4[:, :, :, None, :MLA_NOPE] * eye[None, None, :, :, None]).reshape(n_seq, rows, MLA_HEADS * MLA_NOPE)
    qr = q4[..., MLA_NOPE:MLA_QK].reshape(n_seq, rows, MLA_ROPE)
    tail = _feature_major_tail(lat_new, n_seq)
    seq3 = lambda a: pl.BlockSpec((1,) + a.shape[1:], lambda s, pt: (s, 0, 0))
    full = lambda a: pl.BlockSpec(a.shape, lambda s, pt: (0,) * a.ndim)
    out = pl.pallas_call(
        functools.partial(_mla_sample_kernel, li=li, n_sub=n_sub, ppc=ppc, t_new=t_new),
        grid_spec=pltpu.PrefetchScalarGridSpec(
            num_scalar_prefetch=1, grid=(n_seq,),
            in_specs=[pl.BlockSpec(memory_space=pl.ANY), seq3(qn), seq3(qr), seq3(tail), full(wukt), full(wuv)],
            out_specs=pl.BlockSpec((1, t_new, 512), lambda s, pt: (s, 0, 0)),
            scratch_shapes=[pltpu.VMEM((2, n_sub, MLA_LAT, ppc * page), F32), pltpu.SemaphoreType.DMA((2,)),
                            pltpu.VMEM((rows, 1), F32), pltpu.VMEM((rows, 1), F32),
                            pltpu.VMEM((rows, MLA_KV_LORA), F32)]),
        out_shape=jax.ShapeDtypeStruct((n_seq, t_new, 512), BF16),
        compiler_params=_cparams(("arbitrary",)),
        name="mla_sample_attention",
    )(page_table, cache_t, qn, qr, tail, wukt, wuv)
    return out.reshape(n_seq * t_new, 512)


def _dsa_sample_kernel(pt_ref, idx_cache, kv_cache, iq_ref, iw_ref, q_ref, itail_ref, ktail_ref, o_ref,
                       ibuf, kbuf, isem, ksem, key_ref, tkey_ref, m_ref, l_ref, acc_ref,
                       *, li, n_sub, ppc, t_new, topk, idx_bits):
    s = pl.program_id(0)
    slot = s % 2
    idma = functools.partial(_page_dma, idx_cache, li, pt_ref, page0=0, n_sub=n_sub, ppc=ppc, buf_ref=ibuf, sem=isem)
    kdma = functools.partial(_page_dma, kv_cache, li, pt_ref, page0=0, n_sub=n_sub, ppc=ppc, buf_ref=kbuf, sem=ksem)

    @pl.when(s == 0)
    def _():
        idma(seq=s, slot=slot, wait=False)
        kdma(seq=s, slot=slot, wait=False)

    @pl.when(s + 1 < pl.num_programs(0))
    def _():
        idma(seq=s + 1, slot=1 - slot, wait=False)
        kdma(seq=s + 1, slot=1 - slot, wait=False)

    idma(seq=s, slot=slot, wait=True)

    rows = t_new * IDX_HEADS
    tk = ibuf.shape[-1]
    n_past = n_sub * tk
    iq = iq_ref[0]
    iw = iw_ref[0] * IDX_SCALE
    qrow = lax.broadcasted_iota(jnp.int32, (8, 1), 0)

    def keys_of(ik_t, admissible):
        w = jnp.maximum(_dot(iq, ik_t.astype(BF16)), 0.0) * iw
        width = w.shape[-1]
        sc = jnp.sum(w.reshape(t_new, IDX_HEADS, width), axis=1)
        sc = jnp.concatenate([sc, jnp.zeros((8 - t_new, width), F32)], axis=0)
        return jnp.where(admissible, _sort_key(sc), INT_MIN)

    def score_sub(sub, carry):
        key_ref[sub] = keys_of(ibuf[slot, sub], qrow < t_new)
        return carry

    lax.fori_loop(0, n_sub, score_sub, 0)
    tail_ok = (_lane((8, LANES)) <= qrow) & (qrow < t_new)
    tkey_ref[...] = keys_of(itail_ref[0], tail_ok)

    lane_main = _lane((8, tk))
    lane_tail = n_past + _lane((8, LANES))

    def count(f):
        def body(c, acc):
            return acc + _row_count(f(key_ref[c], c * tk + lane_main))
        return lax.fori_loop(0, n_sub, body, _row_count(f(tkey_ref[...], lane_tail)))

    thr, bound = _topk_threshold(count, 8, topk, idx_bits)

    kdma(seq=s, slot=slot, wait=True)
    q = q_ref[0]
    _init_softmax(m_ref, l_ref, acc_ref)

    def attend(key, idx, kv_t):
        sel = _selected(key, idx, thr, bound)
        bias = jnp.where(sel, 0.0, NEG)
        width = bias.shape[-1]
        bias = jnp.concatenate([jnp.broadcast_to(bias[qq:qq + 1], (DSA_HEADS, width)) for qq in range(t_new)], axis=0)
        sc = _dot(q, kv_t[0:128].astype(BF16)) + bias
        _softmax_step_nt(sc, kv_t[128:256].astype(BF16), m_ref, l_ref, acc_ref)

    def attend_sub(sub, carry):
        attend(key_ref[sub], sub * tk + lane_main, kbuf[slot, sub])
        return carry

    lax.fori_loop(0, n_sub, attend_sub, 0)
    attend(tkey_ref[...], lane_tail, ktail_ref[0])
    o_ref[0] = acc_ref[...] / l_ref[...]


def _dsa_sample(page_table, idx_t, kv_t, li, qd, iq, misc, kv_new, topk, n_seq):
    t_new = qd.shape[0] // n_seq
    rows = t_new * DSA_HEADS
    n_pages = page_table.shape[1]
    page = kv_t.shape[-1]
    ppc = _sub_pages(n_pages, 8)
    n_sub = n_pages // ppc
    tk = ppc * page
    idx_bits = int(n_pages * page + LANES - 1).bit_length()
    q = qd.reshape(n_seq, rows, LANES)
    iqr = iq.reshape(n_seq, rows, LANES)[..., :IDX_DIM]
    iw = misc[:, M_IW:M_IW + IDX_HEADS].reshape(n_seq, rows, 1)
    itail = _feature_major_tail(misc[:, 0:IDX_DIM], n_seq)
    ktail = _feature_major_tail(kv_new, n_seq)
    seq3 = lambda a: pl.BlockSpec((1,) + a.shape[1:], lambda s, pt: (s, 0, 0))
    out = pl.pallas_call(
        functools.partial(_dsa_sample_kernel, li=li, n_sub=n_sub, ppc=ppc, t_new=t_new, topk=topk, idx_bits=idx_bits),
        grid_spec=pltpu.PrefetchScalarGridSpec(
            num_scalar_prefetch=1, grid=(n_seq,),
            in_specs=[pl.BlockSpec(memory_space=pl.ANY), pl.BlockSpec(memory_space=pl.ANY),
                      seq3(iqr), seq3(iw), seq3(q), seq3(itail), seq3(ktail)],
            out_specs=pl.BlockSpec((1, rows, LANES), lambda s, pt: (s, 0, 0)),
            scratch_shapes=[pltpu.VMEM((2, n_sub, IDX_DIM, tk), F32), pltpu.VMEM((2, n_sub, 256, tk), F32),
                            pltpu.SemaphoreType.DMA((2,)), pltpu.SemaphoreType.DMA((2,)),
                            pltpu.VMEM((n_sub, 8, tk), jnp.int32), pltpu.VMEM((8, LANES), jnp.int32),
                            pltpu.VMEM((rows, 1), F32), pltpu.VMEM((rows, 1), F32), pltpu.VMEM((rows, LANES), F32)]),
        out_shape=jax.ShapeDtypeStruct((n_seq, rows, LANES), F32),
        compiler_params=_cparams(("arbitrary",)),
        name="dsa_sample_attention",
    )(page_table, idx_t, kv_t, iqr, iw, q, itail, ktail)
    o = out.reshape(n_seq, t_new, DSA_HEADS, 2, HEAD_DIM)
    heads = jnp.arange(DSA_HEADS)
    o = o[:, :, heads, heads // DSA_GROUP, :]
    return o.reshape(n_seq * t_new, DSA_HEADS * HEAD_DIM).astype(BF16)


def _fox_sample_kernel(pt_ref, kv_cache, lf_cache, q_ref, ktail_ref, ltail_ref, o_ref,
                       kbuf, lbuf, ksem, lsem, m_ref, l_ref, acc_ref, carry_ref, colg_ref,
                       *, li, n_sub, ppc, n_chunk, t_new):
    s, c = pl.program_id(0), pl.program_id(1)
    step = s * n_chunk + c
    slot = step % 2
    cpp = n_sub * ppc

    def dma(seq, chunk, slot, wait):
        page0 = (n_chunk - 1 - chunk) * cpp
        _page_dma(kv_cache, li, pt_ref, seq, page0, n_sub, ppc, kbuf, slot, ksem, wait)
        _page_dma(lf_cache, li, pt_ref, seq, page0, n_sub, ppc, lbuf, slot, lsem, wait)

    @pl.when(step == 0)
    def _():
        dma(s, c, slot, False)

    @pl.when(step + 1 < pl.num_programs(0) * n_chunk)
    def _():
        nxt = step + 1
        dma(nxt // n_chunk, nxt % n_chunk, 1 - slot, False)

    rows = t_new * FOX_HEADS
    tk = kbuf.shape[-1]
    q = q_ref[0]
    later = (lax.broadcasted_iota(jnp.int32, (tk, tk), 0) > lax.broadcasted_iota(jnp.int32, (tk, tk), 1)).astype(BF16)

    def block(kv_t, lf_t, mask, first):
        w = lf_t.shape[-1]
        parts = jnp.concatenate(_split3(lf_t), axis=0)
        sfx = _dot(parts, later[0:w, 0:w])
        sfx = sfx[0:FOX_HEADS] + sfx[FOX_HEADS:2 * FOX_HEADS] + sfx[2 * FOX_HEADS:3 * FOX_HEADS]
        g = sfx + carry_ref[...]
        carry_ref[...] = carry_ref[...] + jnp.sum(lf_t, axis=-1, keepdims=True)
        if first:
            colg_ref[...] = jnp.concatenate([g[:, qq:qq + 1] for qq in range(t_new)], axis=0)
        sc = _dot(q, kv_t[0:512].astype(BF16)) + jnp.concatenate([g] * t_new, axis=0) - colg_ref[...]
        if mask is not None:
            sc = jnp.where(mask, sc, NEG)
        _softmax_step_nt(sc, kv_t[512:1024].astype(BF16), m_ref, l_ref, acc_ref)

    @pl.when(c == 0)
    def _():
        _init_softmax(m_ref, l_ref, acc_ref)
        carry_ref[...] = jnp.zeros(carry_ref.shape, F32)
        block(ktail_ref[0], ltail_ref[0], _tail_mask(rows, FOX_HEADS), True)

    dma(s, c, slot, True)

    def sub_block(i, carry):
        sub = n_sub - 1 - i
        block(kbuf[slot, sub], lbuf[slot, sub], None, False)
        return carry

    lax.fori_loop(0, n_sub, sub_block, 0)

    @pl.when(c == n_chunk - 1)
    def _():
        o_ref[0] = acc_ref[...] / l_ref[...]


def _fox_sample(page_table, kv_t, lf_t, li, q, kv_new, lf_new, n_seq):
    t_new = q.shape[0] // n_seq
    rows = t_new * FOX_HEADS
    n_pages = page_table.shape[1]
    page = kv_t.shape[-1]
    ppc = _sub_pages(n_pages, 2)
    cpp = _sub_pages(n_pages, 16)
    n_sub, n_chunk = cpp // ppc, n_pages // cpp
    tk = ppc * page
    heads = jnp.arange(FOX_HEADS)
    place = (heads[:, None] // 4 == jnp.arange(4)[None, :]).astype(q.dtype)
    q5 = q.reshape(n_seq, t_new, FOX_HEADS, 1, LANES) * place[None, None, :, :, None]
    qbd = q5.reshape(n_seq, rows, 4 * LANES)
    ktail = _feature_major_tail(kv_new, n_seq)
    ltail = _feature_major_tail(lf_new, n_seq)
    seq3 = lambda a: pl.BlockSpec((1,) + a.shape[1:], lambda s, c, pt: (s, 0, 0))
    out = pl.pallas_call(
        functools.partial(_fox_sample_kernel, li=li, n_sub=n_sub, ppc=ppc, n_chunk=n_chunk, t_new=t_new),
        grid_spec=pltpu.PrefetchScalarGridSpec(
            num_scalar_prefetch=1, grid=(n_seq, n_chunk),
            in_specs=[pl.BlockSpec(memory_space=pl.ANY), pl.BlockSpec(memory_space=pl.ANY),
                      seq3(qbd), seq3(ktail), seq3(ltail)],
            out_specs=pl.BlockSpec((1, rows, 512), lambda s, c, pt: (s, 0, 0)),
            scratch_shapes=[pltpu.VMEM((2, n_sub, 1024, tk), F32), pltpu.VMEM((2, n_sub, FOX_HEADS, tk), F32),
                            pltpu.SemaphoreType.DMA((2,)), pltpu.SemaphoreType.DMA((2,)),
                            pltpu.VMEM((rows, 1), F32), pltpu.VMEM((rows, 1), F32), pltpu.VMEM((rows, 512), F32),
                            pltpu.VMEM((FOX_HEADS, 1), F32), pltpu.VMEM((rows, 1), F32)]),
        out_shape=jax.ShapeDtypeStruct((n_seq, rows, 512), F32),
        compiler_params=_cparams(("arbitrary", "arbitrary")),
        name="fox_sample_attention",
    )(page_table, kv_t, lf_t, qbd, ktail, ltail)
    o = out.reshape(n_seq, t_new, FOX_HEADS, FOX_KV_HEADS, HEAD_DIM)
    o = o[:, :, heads, heads // 2, :]
    return o.reshape(n_seq * t_new, FOX_HEADS * HEAD_DIM).astype(BF16)


def _prompt_even(xp, t_real, tab, g_mix, ew, topk):
    qm, km, vm, lat, qd, kvd, kdb, iq, misc = _even_project(xp, t_real, tab, g_mix, ew)
    om = _mla_prompt(qm, km, vm)
    od = _dsa_prompt(qd, iq, misc, kdb, topk)
    return om, od, lat, kvd, misc[:, :t_real, 0:IDX_DIM]


def _prompt_odd(xp, t_real, g_mix, ow):
    q, kv, kvb, lf, c, ct = _odd_project(xp, t_real, g_mix, ow)
    return _fox_prompt(q, kvb, c, ct), kv, lf


def kernel(x_prompt, x_sample, cache_mla, cache_dsa_kv, cache_dsa_idx, cache_fox_kv, cache_fox_logf, page_table, meta_tokens, g_mix, g_ffn, w_in_even, g_mla_q_lat, w_mla_uq, g_mla_kv_lat, w_mla_uk, w_mla_uv, g_mla_q, g_mla_k, g_dsa_q, g_dsa_k, g_idx_k, w_out_even, w_in_odd, b_fox_f, g_fox_q, g_fox_k, w_out_odd, w_ffn_gate, w_ffn_up, w_ffn_down):
    b, s, d = x_prompt.shape
    depth = g_mix.shape[0]
    t_real = s + N_META
    tp = -(-t_real // LANES) * LANES
    meta = jnp.broadcast_to(meta_tokens.astype(x_prompt.dtype)[None], (b, N_META, d))
    xp = jnp.concatenate([meta, x_prompt, jnp.zeros((b, tp - t_real, d), x_prompt.dtype)], axis=1)
    tab_p = _rope_tables(jnp.arange(tp))
    topk_p = min(TOPK_MAX, s // 4)

    n_seq, t_new, _ = x_sample.shape
    n_rows = n_seq * t_new
    past = page_table.shape[1] * cache_mla.shape[2]
    xs = x_sample.reshape(1, n_rows, d)
    tab_s = _rope_tables(past + jnp.arange(n_rows) % t_new)
    topk_s = min(TOPK_MAX, (past + t_new) // 4)
    mla_t = jnp.swapaxes(cache_mla, 2, 3)
    dkv_t = jnp.transpose(cache_dsa_kv, (0, 1, 3, 4, 5, 2)).reshape(cache_dsa_kv.shape[:2] + (256, -1))
    idx_t = jnp.swapaxes(cache_dsa_idx, 2, 3)
    fkv_t = jnp.transpose(cache_fox_kv, (0, 1, 3, 4, 5, 2)).reshape(cache_fox_kv.shape[:2] + (1024, -1))
    flf_t = jnp.swapaxes(cache_fox_logf, 2, 3)

    mla_p, dkv_p, idx_p, fkv_p, lf_p = [], [], [], [], []
    mla_s, dkv_s, idx_s, fkv_s, lf_s = [], [], [], [], []
    for l in range(depth):
        li = l // 2
        wg, wu, wd = w_ffn_gate[l].astype(BF16), w_ffn_up[l].astype(BF16), w_ffn_down[l].astype(BF16)
        if l % 2 == 0:
            ew = _even_weights(w_in_even[li], g_mla_q_lat[li], w_mla_uq[li], g_mla_kv_lat[li], w_mla_uk[li],
                               w_mla_uv[li], g_mla_q[li], g_mla_k[li], g_dsa_q[li], g_dsa_k[li], g_idx_k[li])
            w_out = w_out_even[li].astype(BF16)
            w_outs = (w_out[:512], w_out[512:])
            om, od, lat, kvd, ik = _prompt_even(xp, t_real, tab_p, g_mix[l], ew, topk_p)
            mla_p.append(lat)
            dkv_p.append(kvd.reshape(b, t_real, 2, DSA_KV_HEADS, HEAD_DIM))
            idx_p.append(ik)
            xp = _mix_ffn(xp, (om, od), w_outs, g_ffn[l], wg, wu, wd)

            qm, _, _, lat, qd, kvd, _, iq, misc = _even_project(xs, n_rows, tab_s, g_mix[l], ew)
            wukt = jnp.transpose(w_mla_uk[li], (1, 2, 0)).reshape(MLA_HEADS * MLA_NOPE, MLA_KV_LORA).astype(BF16)
            om = _mla_sample(page_table, mla_t, li, qm[0], lat[0], wukt, ew[5], n_seq)
            od = _dsa_sample(page_table, idx_t, dkv_t, li, qd[0], iq[0], misc[0], kvd[0], topk_s, n_seq)
            mla_s.append(lat.reshape(n_seq, t_new, MLA_LAT))
            dkv_s.append(kvd.reshape(n_seq, t_new, 2, DSA_KV_HEADS, HEAD_DIM))
            idx_s.append(misc[0, :, 0:IDX_DIM].reshape(n_seq, t_new, IDX_DIM))
            xs = _mix_ffn(xs, (om[None], od[None]), w_outs, g_ffn[l], wg, wu, wd)
        else:
            ow = _odd_weights(w_in_odd[li], b_fox_f[li], g_fox_q[li], g_fox_k[li])
            w_outs = (w_out_odd[li].astype(BF16),)
            of, kv, lf = _prompt_odd(xp, t_real, g_mix[l], ow)
            fkv_p.append(kv.reshape(b, t_real, 2, FOX_KV_HEADS, HEAD_DIM))
            lf_p.append(lf)
            xp = _mix_ffn(xp, (of,), w_outs, g_ffn[l], wg, wu, wd)

            q, kv, _, lf, _, _ = _odd_project(xs, n_rows, g_mix[l], ow)
            of = _fox_sample(page_table, fkv_t, flf_t, li, q[0], kv[0], lf[0], n_seq)
            fkv_s.append(kv.reshape(n_seq, t_new, 2, FOX_KV_HEADS, HEAD_DIM))
            lf_s.append(lf.reshape(n_seq, t_new, FOX_HEADS))
            xs = _mix_ffn(xs, (of[None],), w_outs, g_ffn[l], wg, wu, wd)
    y_prompt = xp[:, N_META:t_real]
    y_sample = xs.reshape(n_seq, t_new, d)
    return (y_prompt, y_sample, jnp.stack(mla_p), jnp.stack(mla_s), jnp.stack(dkv_p), jnp.stack(dkv_s),
            jnp.stack(idx_p), jnp.stack(idx_s), jnp.stack(fkv_p), jnp.stack(fkv_s), jnp.stack(lf_p), jnp.stack(lf_s))
```

```python
import functools

import numpy as np
import jax
import jax.numpy as jnp
from jax import lax
from jax.experimental import pallas as pl
from jax.experimental.pallas import tpu as pltpu

F32 = jnp.float32
BF16 = jnp.bfloat16

N_META = 16
HEAD_DIM = 64
ROPE_THETA = 10000.0
NORM_EPS = 1e-6
MLA_HEADS = 8
MLA_NOPE = 64
MLA_ROPE = 32
MLA_QK = MLA_NOPE + MLA_ROPE
MLA_V = 64
MLA_Q_LORA = 384
MLA_KV_LORA = 256
MLA_LAT = MLA_KV_LORA + MLA_ROPE
DSA_HEADS = 8
DSA_KV_HEADS = 2
DSA_GROUP = DSA_HEADS // DSA_KV_HEADS
IDX_HEADS = 8
IDX_DIM = 64
TOPK_MAX = 256
IDX_SCALE = (IDX_DIM * IDX_HEADS) ** -0.5
FOX_HEADS = 16
FOX_KV_HEADS = 8

LANES = 128
VMEM_LIMIT = 56 * 1024 * 1024
NEG = -1e30
LOG2E = 1.4426950408889634
INT_MIN = -(2 ** 31)

P_QL, P_CKV, P_DQ, P_DK, P_DV, P_IQ, P_MISC, P_END = 0, 384, 640, 1152, 1280, 1408, 1920, 2048
M_KPE, M_IW = 64, 96


def _pick_tile(n):
    for t in (384, 256, 128):
        if n % t == 0:
            return t
    raise ValueError(f"row count {n} is not a multiple of 128")


def _cparams(sem):
    return pltpu.CompilerParams(dimension_semantics=sem, vmem_limit_bytes=VMEM_LIMIT)


def _lane(shape):
    return lax.broadcasted_iota(jnp.int32, shape, len(shape) - 1)


def _partner(x, d):
    lane = _lane(x.shape)
    return jnp.where((lane & d) != 0, pltpu.roll(x, d, 1), pltpu.roll(x, LANES - d, 1))


def _seg_allsum(x, seg):
    d = seg // 2
    while d >= 1:
        x = x + _partner(x, d)
        d //= 2
    return x


def _swap_halves(x):
    return pltpu.roll(x, 64, 1)


def _rms_rows(x, g):
    ss = jnp.sum(x * x, axis=-1, keepdims=True)
    return x * lax.rsqrt(ss / x.shape[-1] + NORM_EPS) * g


def _dot(a, b):
    return jnp.dot(a, b, preferred_element_type=F32)


def _dot_nt(a, b):
    return lax.dot_general(a, b, (((1,), (1,)), ((), ())), preferred_element_type=F32)


def _even_proj_kernel(x_ref, tab_ref, gmix_ref, win_ref, gql_ref, wuq_ref, gkv_ref, wuk_ref, wuv_ref,
                      gq_ref, gdq_ref, gdk_ref, gmisc_ref,
                      qm_ref, km_ref, vm_ref, lat_ref, qd_ref, kvd_ref, kdb_ref, iq_ref, misc_ref):
    x = x_ref[0]
    tm = x.shape[0]
    h = _rms_rows(x, gmix_ref[...]).astype(BF16)
    p = _dot(h, win_ref[...])
    lane = _lane((tm, LANES))
    lo = lane < 64
    cos64, sin64 = tab_ref[:, 0:128], tab_ref[:, 128:256]
    cosq, sinq = tab_ref[:, 256:384], tab_ref[:, 384:512]
    cosm, sinm = tab_ref[:, 512:640], tab_ref[:, 640:768]

    ql = _rms_rows(p[:, P_QL:P_CKV], gql_ref[...]).astype(BF16)
    qm = _dot(ql, wuq_ref[...])
    gq = gq_ref[...]
    for hh in range(MLA_HEADS):
        c = qm[:, LANES * hh:LANES * (hh + 1)]
        c = c * cosq + _partner(c, 16) * sinq
        ss = jnp.sum(c * c, axis=-1, keepdims=True)
        c = c * lax.rsqrt(ss / MLA_QK + NORM_EPS) * gq
        qm_ref[0, :, LANES * hh:LANES * (hh + 1)] = c.astype(BF16)

    ckv = _rms_rows(p[:, P_CKV:P_DQ], gkv_ref[...])
    lat_ref[0, :, 0:MLA_KV_LORA] = ckv

    m = p[:, P_MISC:P_END]
    ikss = jnp.sum(jnp.where(lo, m * m, 0.0), axis=-1, keepdims=True)
    m = m * jnp.where(lo, lax.rsqrt(ikss / IDX_DIM + NORM_EPS) * gmisc_ref[...], 1.0)
    rot = jnp.where(lo, _partner(m, 32), _partner(m, 16))
    m = m * cosm + rot * sinm
    misc_ref[0] = m
    lat_ref[0, :, MLA_KV_LORA:MLA_LAT] = m[:, M_KPE:M_KPE + MLA_ROPE]

    ckv_b = ckv.astype(BF16)
    kn = _dot(ckv_b, wuk_ref[...])
    kpe = jnp.where((lane >= M_KPE) & (lane < M_KPE + MLA_ROPE), m, 0.0)
    for hh in range(MLA_HEADS):
        c = kn[:, LANES * hh:LANES * (hh + 1)] + kpe
        ss = jnp.sum(c * c, axis=-1, keepdims=True)
        km_ref[0, :, LANES * hh:LANES * (hh + 1)] = (c * lax.rsqrt(ss / MLA_QK + NORM_EPS)).astype(BF16)
    vm_ref[0] = _dot(ckv_b, wuv_ref[...]).astype(BF16)

    gdq = gdq_ref[...]
    for j in range(DSA_HEADS // 2):
        c = p[:, P_DQ + LANES * j:P_DQ + LANES * (j + 1)]
        c = c * lax.rsqrt(_seg_allsum(c * c, 64) / HEAD_DIM + NORM_EPS) * gdq
        c = c * cos64 + _partner(c, 32) * sin64
        sw = _swap_halves(c)
        if (2 * j) // DSA_GROUP == 0:
            a, b = jnp.where(lo, c, 0.0), jnp.where(lo, sw, 0.0)
        else:
            a, b = jnp.where(lo, 0.0, sw), jnp.where(lo, 0.0, c)
        qd_ref[0, :, LANES * (2 * j):LANES * (2 * j + 1)] = a.astype(BF16)
        qd_ref[0, :, LANES * (2 * j + 1):LANES * (2 * j + 2)] = b.astype(BF16)

    c = p[:, P_DK:P_DV]
    c = c * lax.rsqrt(_seg_allsum(c * c, 64) / HEAD_DIM + NORM_EPS) * gdk_ref[...]
    kd = c * cos64 + _partner(c, 32) * sin64
    vd = p[:, P_DV:P_IQ]
    kvd_ref[0, :, 0:128] = kd
    kvd_ref[0, :, 128:256] = vd
    kdb_ref[0, :, 0:128] = kd.astype(BF16)
    kdb_ref[0, :, 128:256] = vd.astype(BF16)
    kdb_ref[0, :, 256:384] = m.astype(BF16)

    for j in range(IDX_HEADS // 2):
        c = p[:, P_IQ + LANES * j:P_IQ + LANES * (j + 1)]
        c = c * cos64 + _partner(c, 32) * sin64
        iq_ref[0, :, LANES * (2 * j):LANES * (2 * j + 1)] = jnp.where(lo, c, 0.0).astype(BF16)
        iq_ref[0, :, LANES * (2 * j + 1):LANES * (2 * j + 2)] = jnp.where(lo, _swap_halves(c), 0.0).astype(BF16)


def _rope_tables(pos):
    pos = pos.astype(F32)[:, None]

    def cs(half):
        inv = ROPE_THETA ** (-jnp.arange(half, dtype=F32) / half)
        ang = pos * inv
        c, s = jnp.cos(ang), jnp.sin(ang)
        return jnp.concatenate([c, c], -1), jnp.concatenate([-s, s], -1)

    c64, s64 = cs(32)
    c32, s32 = cs(16)
    t = pos.shape[0]
    one, zero = jnp.ones((t, 32), F32), jnp.zeros((t, 32), F32)
    cos64, sin64 = jnp.concatenate([c64, c64], -1), jnp.concatenate([s64, s64], -1)
    cosq = jnp.concatenate([one, one, c32, one], -1)
    sinq = jnp.concatenate([zero, zero, s32, zero], -1)
    cosm = jnp.concatenate([c64, c32, one], -1)
    sinm = jnp.concatenate([s64, s32, zero], -1)
    return jnp.concatenate([cos64, sin64, cosq, sinq, cosm, sinm], -1)


def _even_weights(w_in, g_q_lat, w_uq, g_kv_lat, w_uk, w_uv, g_mq, g_mk, g_dq, g_dk, g_ik):
    d = w_in.shape[0]
    o_dsa = MLA_Q_LORA + MLA_KV_LORA + MLA_ROPE
    o_dq, o_dk, o_dv, o_iq = o_dsa, o_dsa + 512, o_dsa + 640, o_dsa + 768
    o_iw, o_ik = o_iq + 512, o_iq + 520
    win = jnp.concatenate([
        w_in[:, 0:640], w_in[:, o_dq:o_iw], w_in[:, o_ik:o_ik + 64], w_in[:, 640:672], w_in[:, o_iw:o_iw + 8],
        jnp.zeros((d, 24), w_in.dtype)], axis=1).astype(BF16)
    wuq = w_uq.reshape(MLA_Q_LORA, MLA_HEADS, MLA_QK)
    wuq = jnp.pad(wuq, ((0, 0), (0, 0), (0, LANES - MLA_QK))).reshape(MLA_Q_LORA, MLA_HEADS * LANES).astype(BF16)
    wuk = jnp.pad(w_uk, ((0, 0), (0, 0), (0, LANES - MLA_NOPE))).reshape(MLA_KV_LORA, MLA_HEADS * LANES).astype(BF16)
    wuv = w_uv.reshape(MLA_KV_LORA, MLA_HEADS * MLA_V).astype(BF16)
    gq = jnp.concatenate([g_mq * g_mk * (MLA_QK ** -0.5 * LOG2E), jnp.zeros((LANES - MLA_QK,), F32)])[None]
    gdq = (jnp.concatenate([g_dq, g_dq]) * (HEAD_DIM ** -0.5 * LOG2E))[None]
    gdk = jnp.concatenate([g_dk, g_dk])[None]
    gmisc = jnp.concatenate([g_ik, jnp.ones((64,), F32)])[None]
    return win, g_q_lat[None], wuq, g_kv_lat[None], wuk, wuv, gq, gdq, gdk, gmisc


def _even_project(x, t_out, tab, g_mix, ew):
    b, tp, d = x.shape
    tm = _pick_tile(tp)
    win, gql, wuq, gkv, wuk, wuv, gq, gdq, gdk, gmisc = ew
    row = lambda w: pl.BlockSpec((1, tm, w), lambda i, j: (i, j, 0))
    full = lambda a: pl.BlockSpec(a.shape, lambda i, j: (0,) * a.ndim)
    outs = [
        jax.ShapeDtypeStruct((b, tp, 1024), BF16), jax.ShapeDtypeStruct((b, tp, 1024), BF16),
        jax.ShapeDtypeStruct((b, tp, 512), BF16), jax.ShapeDtypeStruct((b, t_out, MLA_LAT), F32),
        jax.ShapeDtypeStruct((b, tp, 1024), BF16), jax.ShapeDtypeStruct((b, t_out, 256), F32),
        jax.ShapeDtypeStruct((b, tp, 384), BF16), jax.ShapeDtypeStruct((b, tp, 1024), BF16),
        jax.ShapeDtypeStruct((b, tp, LANES), F32),
    ]
    params = (g_mix[None], win, gql, wuq, gkv, wuk, wuv, gq, gdq, gdk, gmisc)
    return pl.pallas_call(
        _even_proj_kernel,
        grid=(b, tp // tm),
        in_specs=[row(d), pl.BlockSpec((tm, 768), lambda i, j: (j, 0))] + [full(a) for a in params],
        out_specs=[row(1024), row(1024), row(512), row(MLA_LAT), row(1024), row(256), row(384), row(1024), row(LANES)],
        out_shape=outs,
        compiler_params=_cparams(("parallel", "parallel")),
        name="even_project",
    )(x, tab, *params)


def _flash_init(m_ref, l_ref, acc_ref):
    m_ref[...] = jnp.full(m_ref.shape, NEG, F32)
    l_ref[...] = jnp.zeros(l_ref.shape, F32)
    acc_ref[...] = jnp.zeros(acc_ref.shape, F32)


def _flash_step(s, v, m_ref, l_ref, acc_ref, nt=False):
    tk = s.shape[1]
    tiles = [s[:, LANES * j:LANES * (j + 1)] for j in range(tk // LANES)]
    mx = functools.reduce(jnp.maximum, tiles)
    m_prev = m_ref[...]
    m_new = jnp.maximum(m_prev, jnp.max(mx, axis=-1, keepdims=True))
    alpha = jnp.exp2(m_prev - m_new)
    ps = [jnp.exp2(tl - m_new) for tl in tiles]
    l_ref[...] = alpha * l_ref[...] + functools.reduce(jnp.add, ps)
    p = jnp.concatenate(ps, axis=1).astype(BF16)
    pv = _dot_nt(p, v) if nt else _dot(p, v)
    n = pv.shape[1]
    a = alpha if n == LANES else jnp.concatenate([alpha] * (n // LANES), axis=1)
    acc_ref[...] = a * acc_ref[...] + pv
    m_ref[...] = m_new


def _flash_out(l_ref, acc_ref):
    return acc_ref[...] / jnp.sum(l_ref[...], axis=-1, keepdims=True)


def _causal_tile_mask(rows, t):
    r = lax.broadcasted_iota(jnp.int32, (rows, t), 0) % t
    return r >= lax.broadcasted_iota(jnp.int32, (rows, t), 1)


def _mla_prompt_kernel(q_ref, k_ref, v_ref, o_ref, m_ref, l_ref, acc_ref, *, t):
    qi = pl.program_id(1)
    diag = _causal_tile_mask(2 * t, t)
    lo = _lane((t, LANES)) < 64
    for pair in range(MLA_HEADS // 2):
        ha, hb = 2 * pair, 2 * pair + 1
        qa = q_ref[0, :, LANES * ha:LANES * (ha + 1)]
        qb = q_ref[0, :, LANES * hb:LANES * (hb + 1)]
        _flash_init(m_ref, l_ref, acc_ref)

        def step(j, masked, qa=qa, qb=qb, ha=ha, hb=hb, pair=pair):
            rows = pl.ds(pl.multiple_of(j * t, t), t)
            s = jnp.concatenate([_dot_nt(qa, k_ref[0, rows, LANES * ha:LANES * (ha + 1)]),
                                 _dot_nt(qb, k_ref[0, rows, LANES * hb:LANES * (hb + 1)])], axis=0)
            if masked:
                s = jnp.where(diag, s, NEG)
            _flash_step(s, v_ref[0, rows, LANES * pair:LANES * (pair + 1)], m_ref, l_ref, acc_ref)

        lax.fori_loop(0, qi, lambda j, c: (step(j, False), c)[1], 0)
        step(qi, True)
        out = _flash_out(l_ref, acc_ref)
        o_ref[0, :, LANES * pair:LANES * (pair + 1)] = jnp.where(lo, out[0:t], out[t:2 * t]).astype(o_ref.dtype)


def _mla_prompt(qm, km, vm):
    b, tp, _ = qm.shape
    t = _pick_tile(tp)
    return pl.pallas_call(
        functools.partial(_mla_prompt_kernel, t=t),
        grid=(b, tp // t),
        in_specs=[pl.BlockSpec((1, t, 1024), lambda i, j: (i, j, 0)),
                  pl.BlockSpec((1, tp, 1024), lambda i, j: (i, 0, 0)),
                  pl.BlockSpec((1, tp, 512), lambda i, j: (i, 0, 0))],
        out_specs=pl.BlockSpec((1, t, 512), lambda i, j: (i, j, 0)),
        out_shape=jax.ShapeDtypeStruct((b, tp, 512), BF16),
        scratch_shapes=[pltpu.VMEM((2 * t, LANES), F32)] * 3,
        compiler_params=_cparams(("parallel", "arbitrary")),
        name="mla_prompt_attention",
    )(qm, km, vm)


def _sort_key(score):
    bits = pltpu.bitcast(score + 0.0, jnp.int32)
    return jnp.where(bits < 0, bits ^ 0x7FFFFFFF, bits)


def _row_count(pred):
    return jnp.sum(pred.astype(F32), axis=-1, keepdims=True)


def _topk_threshold(count, rows, k, idx_bits):
    kf = jnp.float32(k)
    t0 = jnp.where(count(lambda key, idx: key >= 0) >= kf, 0, INT_MIN).astype(jnp.int32)

    def key_bit(i, t):
        cand = t | (jnp.int32(1) << (30 - i))
        return jnp.where(count(lambda key, idx: key >= cand) >= kf, cand, t)

    thr = lax.fori_loop(0, 31, key_bit, t0)
    n_ge = count(lambda key, idx: key >= thr)
    all_idx = jnp.full((rows, 1), (1 << idx_bits) - 1, jnp.int32)

    def tie_bound():
        need = kf - count(lambda key, idx: key > thr)

        def idx_bit(i, hi):
            cand = hi & ~(jnp.int32(1) << (idx_bits - 1 - i))
            ok = count(lambda key, idx: (key == thr) & (idx <= cand)) >= need
            return jnp.where(ok, cand, hi)

        return lax.fori_loop(0, idx_bits, idx_bit, all_idx)

    has_tie = jnp.max(jnp.where((n_ge > kf) & (thr > INT_MIN), 1.0, 0.0)) > 0.0
    return thr, lax.cond(has_tie, tie_bound, lambda: all_idx)


def _selected(key, idx, thr, bound):
    return ((key > thr) | ((key == thr) & (idx <= bound))) & (key > INT_MIN)


def _dsa_prompt_kernel(q_ref, iq_ref, misc_ref, kdb_ref, o_ref, key_ref, bias_ref, m_ref, l_ref, acc_ref,
                       *, t, topk, idx_bits):
    qi = pl.program_id(1)
    nk = qi + 1
    row_pos = qi * t + lax.broadcasted_iota(jnp.int32, (t, t), 0)
    col_iota = lax.broadcasted_iota(jnp.int32, (t, t), 1)
    iw = misc_ref[0][:, M_IW:M_IW + IDX_HEADS] * IDX_SCALE

    def score_chunk(c, carry):
        rows = pl.ds(pl.multiple_of(c * t, t), t)
        ik = kdb_ref[0, rows, 256:384]
        acc = jnp.zeros((t, t), F32)
        for hh in range(IDX_HEADS):
            d = _dot_nt(iq_ref[0, :, LANES * hh:LANES * (hh + 1)], ik)
            acc = acc + iw[:, hh:hh + 1] * jnp.maximum(d, 0.0)
        ok = row_pos >= c * t + col_iota
        key_ref[c] = jnp.where(ok, _sort_key(acc), INT_MIN)
        return carry

    lax.fori_loop(0, nk, score_chunk, 0)

    def count(f):
        def body(c, acc):
            hit = f(key_ref[c], c * t + col_iota).astype(F32)
            return acc + functools.reduce(jnp.add, [hit[:, LANES * j:LANES * (j + 1)] for j in range(t // LANES)])
        return jnp.sum(lax.fori_loop(0, nk, body, jnp.zeros((t, LANES), F32)), axis=-1, keepdims=True)

    thr, bound = _topk_threshold(count, t, topk, idx_bits)

    _flash_init(m_ref, l_ref, acc_ref)

    def attend_chunk(c, carry):
        rows = pl.ds(pl.multiple_of(c * t, t), t)
        bias_ref[...] = jnp.where(_selected(key_ref[c], c * t + col_iota, thr, bound), 0.0, NEG)
        kc = kdb_ref[0, rows, 0:128]
        vc = kdb_ref[0, rows, 128:256]
        for g in range(DSA_KV_HEADS):
            s = jnp.concatenate([_dot_nt(q_ref[0, :, LANES * hh:LANES * (hh + 1)], kc) + bias_ref[...]
                                 for hh in range(DSA_GROUP * g, DSA_GROUP * (g + 1))], axis=0)
            _flash_step(s, vc, m_ref.at[g], l_ref.at[g], acc_ref.at[g])
        return carry

    lax.fori_loop(0, nk, attend_chunk, 0)

    lo = _lane((t, LANES)) < 64
    for g in range(DSA_KV_HEADS):
        out = _flash_out(l_ref.at[g], acc_ref.at[g])
        for i in range(DSA_GROUP // 2):
            a, b = out[2 * i * t:(2 * i + 1) * t], out[(2 * i + 1) * t:(2 * i + 2) * t]
            pair = jnp.where(lo, a, _swap_halves(b)) if g == 0 else jnp.where(lo, _swap_halves(a), b)
            col = LANES * (g * DSA_GROUP // 2 + i)
            o_ref[0, :, col:col + LANES] = pair.astype(o_ref.dtype)


def _dsa_prompt(qd, iq, misc, kdb, topk):
    b, tp, _ = qd.shape
    t = _pick_tile(tp)
    n = tp // t
    idx_bits = max(1, int(tp - 1).bit_length())
    return pl.pallas_call(
        functools.partial(_dsa_prompt_kernel, t=t, topk=topk, idx_bits=idx_bits),
        grid=(b, n),
        in_specs=[pl.BlockSpec((1, t, 1024), lambda i, j: (i, j, 0)),
                  pl.BlockSpec((1, t, 1024), lambda i, j: (i, j, 0)),
                  pl.BlockSpec((1, t, LANES), lambda i, j: (i, j, 0)),
                  pl.BlockSpec((1, tp, 384), lambda i, j: (i, 0, 0))],
        out_specs=pl.BlockSpec((1, t, 512), lambda i, j: (i, j, 0)),
        out_shape=jax.ShapeDtypeStruct((b, tp, 512), BF16),
        scratch_shapes=[pltpu.VMEM((n, t, t), jnp.int32), pltpu.VMEM((t, t), F32)]
        + [pltpu.VMEM((DSA_KV_HEADS, DSA_GROUP * t, LANES), F32)] * 3,
        compiler_params=_cparams(("parallel", "arbitrary")),
        name="dsa_prompt_attention",
    )(qd, iq, misc, kdb)


def _mix_ffn_kernel(*refs, n_mix):
    x_ref = refs[0]
    o_refs = refs[1:1 + n_mix]
    w_refs = refs[1 + n_mix:1 + 2 * n_mix]
    g_ref, wg_ref, wu_ref, wd_ref, y_ref = refs[1 + 2 * n_mix:]
    x = x_ref[0]
    for o_ref, w_ref in zip(o_refs, w_refs):
        x = x + _dot(o_ref[0], w_ref[...])
    h = _rms_rows(x, g_ref[...]).astype(BF16)
    gate = _dot(h, wg_ref[...])
    up = _dot(h, wu_ref[...])
    act = (gate * jax.nn.sigmoid(gate) * up).astype(BF16)
    y_ref[0] = x + _dot(act, wd_ref[...])


def _mix_ffn(x, mixes, w_outs, g, wg, wu, wd):
    b, tp, d = x.shape
    tm = _pick_tile(tp)
    n_mix = len(mixes)
    row = lambda w: pl.BlockSpec((1, tm, w), lambda i, j: (i, j, 0))
    full = lambda a: pl.BlockSpec(a.shape, lambda i, j: (0,) * a.ndim)
    params = (g[None], wg, wu, wd)
    return pl.pallas_call(
        functools.partial(_mix_ffn_kernel, n_mix=n_mix),
        grid=(b, tp // tm),
        in_specs=[row(d)] + [row(o.shape[-1]) for o in mixes] + [full(w) for w in w_outs] + [full(a) for a in params],
        out_specs=row(d),
        out_shape=jax.ShapeDtypeStruct((b, tp, d), F32),
        compiler_params=_cparams(("parallel", "parallel")),
        name="mix_ffn",
    )(x, *mixes, *w_outs, *params)


O_Q, O_K, O_V, O_F, O_END = 0, 1024, 1536, 2048, 2176


def _split3(x):
    a = x.astype(BF16)
    r = x - a.astype(F32)
    b = r.astype(BF16)
    c = (r - b.astype(F32)).astype(BF16)
    return a, b, c


FQ_W, FK_W, FV_OFF, FKV_W = 256, 256, 1024, 1536


def _odd_proj_kernel(x_ref, gmix_ref, win_ref, bf_ref, gq_ref, gk_ref,
                     q_ref, kv_ref, kvb_ref, lf_ref, carry_ref):
    @pl.when(pl.program_id(1) == 0)
    def _():
        carry_ref[...] = jnp.zeros(carry_ref.shape, F32)

    x = x_ref[0]
    tm = x.shape[0]
    h = _rms_rows(x, gmix_ref[...]).astype(BF16)
    p = _dot(h, win_ref[...])
    lane = _lane((tm, LANES))
    lo = lane < 64

    lf = jax.nn.log_sigmoid(p[:, O_F:O_END] + bf_ref[...])
    lf_ref[0] = lf[:, 0:FOX_HEADS]
    tri = (lax.broadcasted_iota(jnp.int32, (tm, tm), 0) >= lax.broadcasted_iota(jnp.int32, (tm, tm), 1)).astype(BF16)
    cum = carry_ref[0:1, :]
    for part in _split3(lf):
        cum = cum + _dot(tri, part)
    carry_ref[0:1, :] = cum[tm - 1:tm, :]
    heads = lane < FOX_HEADS
    c2 = cum * LOG2E
    hi = jnp.where(heads, c2.astype(BF16).astype(F32), 0.0)
    r1 = c2 - hi
    mid = jnp.where(heads, r1.astype(BF16).astype(F32), 0.0)
    low = jnp.where(heads, (r1 - mid).astype(BF16).astype(F32), 0.0)
    packed = hi + pltpu.roll(mid, 16, 1) + pltpu.roll(low, 32, 1)
    bias_key = (packed + jnp.where((lane >= 64) & (lane < 112), 1.0, 0.0)).astype(BF16)
    cq = _swap_halves(packed)

    gq, gk = gq_ref[...], gk_ref[...]
    for j in range(FOX_HEADS // 2):
        c = p[:, O_Q + LANES * j:O_Q + LANES * (j + 1)]
        c = c * lax.rsqrt(_seg_allsum(c * c, 64) / HEAD_DIM + NORM_EPS) * gq
        sw = _swap_halves(c)
        if j % 2 == 0:
            pair = (jnp.where(lo, c, 0.0), jnp.where(lo, sw, 0.0))
        else:
            pair = (jnp.where(lo, 0.0, sw), jnp.where(lo, 0.0, c))
        for i, qh in enumerate(pair):
            hh = 2 * j + i
            minus = (lane == hh) | (lane == 16 + hh) | (lane == 32 + hh)
            own = (lane == 64 + hh) | (lane == 80 + hh) | (lane == 96 + hh)
            q_ref[0, :, FQ_W * hh:FQ_W * hh + LANES] = qh.astype(BF16)
            q_ref[0, :, FQ_W * hh + LANES:FQ_W * (hh + 1)] = jnp.where(minus, -1.0, jnp.where(own, cq, 0.0)).astype(BF16)
    for j in range(FOX_KV_HEADS // 2):
        c = p[:, O_K + LANES * j:O_K + LANES * (j + 1)]
        c = c * lax.rsqrt(_seg_allsum(c * c, 64) / HEAD_DIM + NORM_EPS) * gk
        kv_ref[0, :, LANES * j:LANES * (j + 1)] = c
        kvb_ref[0, :, FK_W * j:FK_W * j + LANES] = c.astype(BF16)
        kvb_ref[0, :, FK_W * j + LANES:FK_W * (j + 1)] = bias_key
    v = p[:, O_V:O_F]
    kv_ref[0, :, 512:1024] = v
    kvb_ref[0, :, FV_OFF:FKV_W] = v.astype(BF16)


def _odd_weights(w_in, b_f, g_q, g_k):
    d = w_in.shape[0]
    win = jnp.concatenate([w_in, jnp.zeros((d, O_END - w_in.shape[1]), w_in.dtype)], axis=1).astype(BF16)
    bf = jnp.concatenate([b_f, jnp.zeros((LANES - FOX_HEADS,), F32)])[None]
    gq = (jnp.concatenate([g_q, g_q]) * (HEAD_DIM ** -0.5 * LOG2E))[None]
    gk = jnp.concatenate([g_k, g_k])[None]
    return win, bf, gq, gk


def _odd_project(x, t_out, g_mix, ow):
    b, tp, d = x.shape
    tm = _pick_tile(tp)
    n = tp // tm
    win, bf, gq, gk = ow
    row = lambda w: pl.BlockSpec((1, tm, w), lambda i, j: (i, j, 0))
    full = lambda a: pl.BlockSpec(a.shape, lambda i, j: (0,) * a.ndim)
    params = (g_mix[None], win, bf, gq, gk)
    outs = [
        jax.ShapeDtypeStruct((b, tp, FOX_HEADS * FQ_W), BF16), jax.ShapeDtypeStruct((b, t_out, 1024), F32),
        jax.ShapeDtypeStruct((b, tp, FKV_W), BF16), jax.ShapeDtypeStruct((b, t_out, FOX_HEADS), F32),
    ]
    return pl.pallas_call(
        _odd_proj_kernel,
        grid=(b, n),
        in_specs=[row(d)] + [full(a) for a in params],
        out_specs=[row(FOX_HEADS * FQ_W), row(1024), row(FKV_W), row(FOX_HEADS)],
        out_shape=outs,
        scratch_shapes=[pltpu.VMEM((8, LANES), F32)],
        compiler_params=_cparams(("parallel", "arbitrary")),
        name="odd_project",
    )(x, *params)


def _fox_prompt_kernel(q_ref, kv_ref, o_ref, m_ref, l_ref, acc_ref, *, t):
    qi = pl.program_id(1)
    diag = _causal_tile_mask(2 * t, t)
    lo = _lane((t, LANES)) < 64
    for kvh in range(FOX_KV_HEADS):
        grp, half = kvh // 2, kvh % 2
        _flash_init(m_ref, l_ref, acc_ref)

        def step(j, masked, kvh=kvh, grp=grp):
            rows = pl.ds(pl.multiple_of(j * t, t), t)
            k = kv_ref[0, rows, FK_W * grp:FK_W * (grp + 1)]
            s = jnp.concatenate([_dot_nt(q_ref[0, :, FQ_W * hh:FQ_W * (hh + 1)], k) for hh in (2 * kvh, 2 * kvh + 1)],
                                axis=0)
            if masked:
                s = jnp.where(diag, s, NEG)
            _flash_step(s, kv_ref[0, rows, FV_OFF + LANES * grp:FV_OFF + LANES * (grp + 1)], m_ref, l_ref, acc_ref)

        lax.fori_loop(0, qi, lambda j, c: (step(j, False), c)[1], 0)
        step(qi, True)
        out = _flash_out(l_ref, acc_ref)
        a, b = out[0:t], out[t:2 * t]
        pair = jnp.where(lo, a, _swap_halves(b)) if half == 0 else jnp.where(lo, _swap_halves(a), b)
        o_ref[0, :, LANES * kvh:LANES * (kvh + 1)] = pair.astype(o_ref.dtype)


def _fox_prompt(q, kvb):
    b, tp, _ = q.shape
    t = _pick_tile(tp)
    return pl.pallas_call(
        functools.partial(_fox_prompt_kernel, t=t),
        grid=(b, tp // t),
        in_specs=[pl.BlockSpec((1, t, FOX_HEADS * FQ_W), lambda i, j: (i, j, 0)),
                  pl.BlockSpec((1, tp, FKV_W), lambda i, j: (i, 0, 0))],
        out_specs=pl.BlockSpec((1, t, 1024), lambda i, j: (i, j, 0)),
        out_shape=jax.ShapeDtypeStruct((b, tp, 1024), BF16),
        scratch_shapes=[pltpu.VMEM((2 * t, LANES), F32)] * 3,
        compiler_params=_cparams(("parallel", "arbitrary")),
        name="fox_prompt_attention",
    )(q, kvb)


def _page_dma(cache_ref, li, pt_ref, seq, page0, n_sub, ppc, buf_ref, slot, sem, wait):
    width = cache_ref.shape[-1]

    def body(sub, carry):
        for pp in range(ppc):
            page = pt_ref[seq, page0 + sub * ppc + pp]
            cp = pltpu.make_async_copy(cache_ref.at[li, page],
                                       buf_ref.at[slot, sub, :, pl.ds(pp * width, width)], sem.at[slot])
            if wait:
                cp.wait()
            else:
                cp.start()
        return carry

    lax.fori_loop(0, n_sub, body, 0)


def _tail_mask(rows, heads):
    r = lax.broadcasted_iota(jnp.int32, (rows, LANES), 0)
    return _lane((rows, LANES)) <= r // heads


def _feature_major_tail(x, n_seq):
    t_new = x.shape[0] // n_seq
    xt = jnp.swapaxes(x.reshape(n_seq, t_new, x.shape[1]), 1, 2)
    return jnp.pad(xt, ((0, 0), (0, 0), (0, LANES - t_new)))


def _mla_sample_kernel(pt_ref, cache_ref, qn_ref, qr_ref, tail_ref, wukt_ref, wuv_ref, o_ref,
                       buf, sem, m_ref, l_ref, acc_ref, *, li, n_sub, ppc, t_new):
    s = pl.program_id(0)
    slot = s % 2
    dma = functools.partial(_page_dma, cache_ref, li, pt_ref, page0=0, n_sub=n_sub, ppc=ppc, buf_ref=buf, sem=sem)

    @pl.when(s == 0)
    def _():
        dma(seq=s, slot=slot, wait=False)

    @pl.when(s + 1 < pl.num_programs(0))
    def _():
        dma(seq=s + 1, slot=1 - slot, wait=False)

    dma(seq=s, slot=slot, wait=True)

    rows = t_new * MLA_HEADS
    wukt = wukt_ref[...]
    qabs = _dot(qn_ref[0], wukt).astype(BF16)
    qr = qr_ref[0]
    _flash_init(m_ref, l_ref, acc_ref)

    def block(lat_t, mask):
        ckv = lat_t[0:MLA_KV_LORA].astype(BF16)
        kpe = lat_t[MLA_KV_LORA:MLA_LAT]
        kn = _dot(wukt, ckv)
        tk = kn.shape[-1]
        ss = jnp.sum((kn * kn).reshape(MLA_HEADS, MLA_NOPE, tk), axis=1)
        ss = ss + jnp.sum(kpe * kpe, axis=0, keepdims=True)
        rs = lax.rsqrt(ss / MLA_QK + NORM_EPS)
        sc = _dot(qabs, ckv) + _dot(qr, kpe.astype(BF16))
        sc = sc * jnp.concatenate([rs] * t_new, axis=0)
        if mask is not None:
            sc = jnp.where(mask, sc, NEG)
        _flash_step(sc, ckv, m_ref, l_ref, acc_ref, nt=True)

    def main(sub, carry):
        block(buf[slot, sub], None)
        return carry

    lax.fori_loop(0, n_sub, main, 0)
    block(tail_ref[0], _tail_mask(rows, MLA_HEADS))

    o_lat = _flash_out(l_ref, acc_ref).astype(BF16)
    o = _dot(o_lat, wuv_ref[...])
    own = lax.broadcasted_iota(jnp.int32, (MLA_HEADS, 512), 1) // MLA_V == \
        lax.broadcasted_iota(jnp.int32, (MLA_HEADS, 512), 0)
    for q in range(t_new):
        blk = jnp.where(own, o[MLA_HEADS * q:MLA_HEADS * (q + 1)], 0.0)
        o_ref[0, q:q + 1, :] = jnp.sum(blk, axis=0, keepdims=True).astype(o_ref.dtype)


def _sub_pages(n_pages, want):
    ppc = want
    while n_pages % ppc:
        ppc //= 2
    return ppc


def _mla_sample(page_table, cache_t, li, qm, lat_new, wukt, wuv, n_seq):
    t_new = qm.shape[0] // n_seq
    rows = t_new * MLA_HEADS
    n_pages = page_table.shape[1]
    page = cache_t.shape[-1]
    ppc = _sub_pages(n_pages, 8)
    n_sub = n_pages // ppc
    q4 = qm.reshape(n_seq, t_new, MLA_HEADS, LANES)
    eye = jnp.eye(MLA_HEADS, dtype=qm.dtype)
    qn = (q4[:, :, :, None, :MLA_NOPE] * eye[None, None, :, :, None]).reshape(n_seq, rows, MLA_HEADS * MLA_NOPE)
    qr = q4[..., MLA_NOPE:MLA_QK].reshape(n_seq, rows, MLA_ROPE)
    tail = _feature_major_tail(lat_new, n_seq)
    seq3 = lambda a: pl.BlockSpec((1,) + a.shape[1:], lambda s, pt: (s, 0, 0))
    full = lambda a: pl.BlockSpec(a.shape, lambda s, pt: (0,) * a.ndim)
    out = pl.pallas_call(
        functools.partial(_mla_sample_kernel, li=li, n_sub=n_sub, ppc=ppc, t_new=t_new),
        grid_spec=pltpu.PrefetchScalarGridSpec(
            num_scalar_prefetch=1, grid=(n_seq,),
            in_specs=[pl.BlockSpec(memory_space=pl.ANY), seq3(qn), seq3(qr), seq3(tail), full(wukt), full(wuv)],
            out_specs=pl.BlockSpec((1, t_new, 512), lambda s, pt: (s, 0, 0)),
            scratch_shapes=[pltpu.VMEM((2, n_sub, MLA_LAT, ppc * page), F32), pltpu.SemaphoreType.DMA((2,)),
                            pltpu.VMEM((rows, LANES), F32), pltpu.VMEM((rows, LANES), F32),
                            pltpu.VMEM((rows, MLA_KV_LORA), F32)]),
        out_shape=jax.ShapeDtypeStruct((n_seq, t_new, 512), BF16),
        compiler_params=_cparams(("arbitrary",)),
        name="mla_sample_attention",
    )(page_table, cache_t, qn, qr, tail, wukt, wuv)
    return out.reshape(n_seq * t_new, 512)


def _dsa_sample_kernel(pt_ref, idx_cache, kv_cache, iq_ref, iw_ref, q_ref, itail_ref, ktail_ref, o_ref,
                       ibuf, kbuf, isem, ksem, key_ref, m_ref, l_ref, acc_ref,
                       *, li, n_sub, ppc, t_new, topk, idx_bits):
    s = pl.program_id(0)
    slot = s % 2
    idma = functools.partial(_page_dma, idx_cache, li, pt_ref, page0=0, n_sub=n_sub, ppc=ppc, buf_ref=ibuf, sem=isem)
    kdma = functools.partial(_page_dma, kv_cache, li, pt_ref, page0=0, n_sub=n_sub, ppc=ppc, buf_ref=kbuf, sem=ksem)

    @pl.when(s == 0)
    def _():
        idma(seq=s, slot=slot, wait=False)
        kdma(seq=s, slot=slot, wait=False)

    @pl.when(s + 1 < pl.num_programs(0))
    def _():
        idma(seq=s + 1, slot=1 - slot, wait=False)
        kdma(seq=s + 1, slot=1 - slot, wait=False)

    idma(seq=s, slot=slot, wait=True)

    rows = t_new * IDX_HEADS
    tk = ibuf.shape[-1]
    n_past = n_sub * tk
    iq = iq_ref[0]
    iw = iw_ref[0] * IDX_SCALE
    qrow = lax.broadcasted_iota(jnp.int32, (8, 1), 0)

    def keys_of(ik_t, admissible):
        w = jnp.maximum(_dot(iq, ik_t.astype(BF16)), 0.0) * iw
        width = w.shape[-1]
        sc = jnp.sum(w.reshape(t_new, IDX_HEADS, width), axis=1)
        sc = jnp.concatenate([sc, jnp.zeros((8 - t_new, width), F32)], axis=0)
        return jnp.where(admissible, _sort_key(sc), INT_MIN)

    for sub in range(n_sub):
        key_ref[:, sub * tk:(sub + 1) * tk] = keys_of(ibuf[slot, sub], qrow < t_new)
    tail_ok = (_lane((8, LANES)) <= qrow) & (qrow < t_new)
    key_ref[:, n_past:n_past + LANES] = keys_of(itail_ref[0], tail_ok)

    def count(f):
        return _row_count(f(key_ref[...], _lane(key_ref.shape)))

    thr, bound = _topk_threshold(count, 8, topk, idx_bits)

    kdma(seq=s, slot=slot, wait=True)
    q = q_ref[0]
    _flash_init(m_ref, l_ref, acc_ref)

    def attend(lane0, width, kv_t):
        key = key_ref[:, lane0:lane0 + width]
        sel = _selected(key, lane0 + _lane((8, width)), thr, bound)
        bias = jnp.where(sel, 0.0, NEG)
        bias = jnp.concatenate([jnp.broadcast_to(bias[qq:qq + 1], (DSA_HEADS, width)) for qq in range(t_new)], axis=0)
        sc = _dot(q, kv_t[0:128].astype(BF16)) + bias
        _flash_step(sc, kv_t[128:256].astype(BF16), m_ref, l_ref, acc_ref, nt=True)

    for sub in range(n_sub):
        attend(sub * tk, tk, kbuf[slot, sub])
    attend(n_past, LANES, ktail_ref[0])
    o_ref[0] = _flash_out(l_ref, acc_ref)


def _dsa_sample(page_table, idx_t, kv_t, li, qd, iq, misc, kv_new, topk, n_seq):
    t_new = qd.shape[0] // n_seq
    rows = t_new * DSA_HEADS
    n_pages = page_table.shape[1]
    page = kv_t.shape[-1]
    ppc = _sub_pages(n_pages, 8)
    n_sub = n_pages // ppc
    tk = ppc * page
    idx_bits = int(n_pages * page + LANES - 1).bit_length()
    q = qd.reshape(n_seq, rows, LANES)
    iqr = iq.reshape(n_seq, rows, LANES)[..., :IDX_DIM]
    iw = misc[:, M_IW:M_IW + IDX_HEADS].reshape(n_seq, rows, 1)
    itail = _feature_major_tail(misc[:, 0:IDX_DIM], n_seq)
    ktail = _feature_major_tail(kv_new, n_seq)
    seq3 = lambda a: pl.BlockSpec((1,) + a.shape[1:], lambda s, pt: (s, 0, 0))
    out = pl.pallas_call(
        functools.partial(_dsa_sample_kernel, li=li, n_sub=n_sub, ppc=ppc, t_new=t_new, topk=topk, idx_bits=idx_bits),
        grid_spec=pltpu.PrefetchScalarGridSpec(
            num_scalar_prefetch=1, grid=(n_seq,),
            in_specs=[pl.BlockSpec(memory_space=pl.ANY), pl.BlockSpec(memory_space=pl.ANY),
                      seq3(iqr), seq3(iw), seq3(q), seq3(itail), seq3(ktail)],
            out_specs=pl.BlockSpec((1, rows, LANES), lambda s, pt: (s, 0, 0)),
            scratch_shapes=[pltpu.VMEM((2, n_sub, IDX_DIM, tk), F32), pltpu.VMEM((2, n_sub, 256, tk), F32),
                            pltpu.SemaphoreType.DMA((2,)), pltpu.SemaphoreType.DMA((2,)),
                            pltpu.VMEM((8, n_sub * tk + LANES), jnp.int32)]
            + [pltpu.VMEM((rows, LANES), F32)] * 3),
        out_shape=jax.ShapeDtypeStruct((n_seq, rows, LANES), F32),
        compiler_params=_cparams(("arbitrary",)),
        name="dsa_sample_attention",
    )(page_table, idx_t, kv_t, iqr, iw, q, itail, ktail)
    o = out.reshape(n_seq, t_new, DSA_HEADS, 2, HEAD_DIM)
    heads = jnp.arange(DSA_HEADS)
    o = o[:, :, heads, heads // DSA_GROUP, :]
    return o.reshape(n_seq * t_new, DSA_HEADS * HEAD_DIM).astype(BF16)


def _fox_sample_kernel(pt_ref, kv_cache, lf_cache, q_ref, ktail_ref, ltail_ref, o_ref,
                       kbuf, lbuf, ksem, lsem, m_ref, l_ref, acc_ref, carry_ref, colg_ref,
                       *, li, n_sub, ppc, n_chunk, t_new):
    s, c = pl.program_id(0), pl.program_id(1)
    step = s * n_chunk + c
    slot = step % 2
    cpp = n_sub * ppc

    def dma(seq, chunk, slot, wait):
        page0 = (n_chunk - 1 - chunk) * cpp
        _page_dma(kv_cache, li, pt_ref, seq, page0, n_sub, ppc, kbuf, slot, ksem, wait)
        _page_dma(lf_cache, li, pt_ref, seq, page0, n_sub, ppc, lbuf, slot, lsem, wait)

    @pl.when(step == 0)
    def _():
        dma(s, c, slot, False)

    @pl.when(step + 1 < pl.num_programs(0) * n_chunk)
    def _():
        nxt = step + 1
        dma(nxt // n_chunk, nxt % n_chunk, 1 - slot, False)

    rows = t_new * FOX_HEADS
    q = q_ref[0]
    sb = min(256, kbuf.shape[-1])
    later = (lax.broadcasted_iota(jnp.int32, (sb, sb), 0) > lax.broadcasted_iota(jnp.int32, (sb, sb), 1)).astype(BF16)

    def later_sums(lf_t):
        w = lf_t.shape[-1]
        bw = min(sb, w)
        parts = jnp.concatenate(_split3(lf_t), axis=0)
        carry = carry_ref[...]
        gs = [None] * (w // bw)
        for jb in reversed(range(w // bw)):
            sfx = _dot(parts[:, jb * bw:(jb + 1) * bw], later[0:bw, 0:bw])
            gs[jb] = sfx[0:FOX_HEADS] + sfx[FOX_HEADS:2 * FOX_HEADS] + sfx[2 * FOX_HEADS:3 * FOX_HEADS] + carry
            carry = carry + jnp.sum(lf_t[:, jb * bw:(jb + 1) * bw], axis=-1, keepdims=True)
        carry_ref[...] = carry
        return jnp.concatenate(gs, axis=1) * LOG2E

    def block(k_t, v_t, lf_t, mask, first):
        g = later_sums(lf_t)
        if first:
            colg_ref[...] = jnp.concatenate([g[:, qq:qq + 1] for qq in range(t_new)], axis=0)
        sc = _dot(q, k_t.astype(BF16)) + jnp.concatenate([g] * t_new, axis=0) - colg_ref[...]
        if mask is not None:
            sc = jnp.where(mask, sc, NEG)
        _flash_step(sc, v_t.astype(BF16), m_ref, l_ref, acc_ref, nt=True)

    @pl.when(c == 0)
    def _():
        _flash_init(m_ref, l_ref, acc_ref)
        carry_ref[...] = jnp.zeros(carry_ref.shape, F32)
        block(ktail_ref[0, 0:512], ktail_ref[0, 512:1024], ltail_ref[0], _tail_mask(rows, FOX_HEADS), True)

    dma(s, c, slot, True)
    block(kbuf[slot, 0, 0:512], kbuf[slot, 0, 512:1024], lbuf[slot, 0], None, False)

    @pl.when(c == n_chunk - 1)
    def _():
        o_ref[0] = _flash_out(l_ref, acc_ref)


def _fox_sample(page_table, kv_t, lf_t, li, q, kv_new, lf_new, n_seq):
    t_new = q.shape[0] // n_seq
    rows = t_new * FOX_HEADS
    n_pages = page_table.shape[1]
    page = kv_t.shape[-1]
    ppc = _sub_pages(n_pages, 16)
    n_sub, n_chunk = 1, n_pages // ppc
    tk = ppc * page
    heads = jnp.arange(FOX_HEADS)
    place = (heads[:, None] // 4 == jnp.arange(4)[None, :]).astype(q.dtype)
    qpad = q.reshape(n_seq, t_new, FOX_HEADS, 1, FQ_W)[..., :LANES]
    qbd = (qpad * place[None, None, :, :, None]).reshape(n_seq, rows, 4 * LANES)
    ktail = _feature_major_tail(kv_new, n_seq)
    ltail = _feature_major_tail(lf_new, n_seq)
    seq3 = lambda a: pl.BlockSpec((1,) + a.shape[1:], lambda s, c, pt: (s, 0, 0))
    out = pl.pallas_call(
        functools.partial(_fox_sample_kernel, li=li, n_sub=n_sub, ppc=ppc, n_chunk=n_chunk, t_new=t_new),
        grid_spec=pltpu.PrefetchScalarGridSpec(
            num_scalar_prefetch=1, grid=(n_seq, n_chunk),
            in_specs=[pl.BlockSpec(memory_space=pl.ANY), pl.BlockSpec(memory_space=pl.ANY),
                      seq3(qbd), seq3(ktail), seq3(ltail)],
            out_specs=pl.BlockSpec((1, rows, 512), lambda s, c, pt: (s, 0, 0)),
            scratch_shapes=[pltpu.VMEM((2, n_sub, 1024, tk), F32), pltpu.VMEM((2, n_sub, FOX_HEADS, tk), F32),
                            pltpu.SemaphoreType.DMA((2,)), pltpu.SemaphoreType.DMA((2,)),
                            pltpu.VMEM((rows, LANES), F32), pltpu.VMEM((rows, LANES), F32),
                            pltpu.VMEM((rows, 512), F32),
                            pltpu.VMEM((FOX_HEADS, 1), F32), pltpu.VMEM((rows, 1), F32)]),
        out_shape=jax.ShapeDtypeStruct((n_seq, rows, 512), F32),
        compiler_params=_cparams(("arbitrary", "arbitrary")),
        name="fox_sample_attention",
    )(page_table, kv_t, lf_t, qbd, ktail, ltail)
    o = out.reshape(n_seq, t_new, FOX_HEADS, FOX_KV_HEADS, HEAD_DIM)
    o = o[:, :, heads, heads // 2, :]
    return o.reshape(n_seq * t_new, FOX_HEADS * HEAD_DIM).astype(BF16)


def _prompt_even(xp, t_real, tab, g_mix, ew, topk):
    qm, km, vm, lat, qd, kvd, kdb, iq, misc = _even_project(xp, t_real, tab, g_mix, ew)
    om = _mla_prompt(qm, km, vm)
    od = _dsa_prompt(qd, iq, misc, kdb, topk)
    return om, od, lat, kvd, misc[:, :t_real, 0:IDX_DIM]


def _prompt_odd(xp, t_real, g_mix, ow):
    q, kv, kvb, lf = _odd_project(xp, t_real, g_mix, ow)
    return _fox_prompt(q, kvb), kv, lf


def kernel(x_prompt, x_sample, cache_mla, cache_dsa_kv, cache_dsa_idx, cache_fox_kv, cache_fox_logf, page_table, meta_tokens, g_mix, g_ffn, w_in_even, g_mla_q_lat, w_mla_uq, g_mla_kv_lat, w_mla_uk, w_mla_uv, g_mla_q, g_mla_k, g_dsa_q, g_dsa_k, g_idx_k, w_out_even, w_in_odd, b_fox_f, g_fox_q, g_fox_k, w_out_odd, w_ffn_gate, w_ffn_up, w_ffn_down):
    b, s, d = x_prompt.shape
    depth = g_mix.shape[0]
    t_real = s + N_META
    tp = -(-t_real // LANES) * LANES
    meta = jnp.broadcast_to(meta_tokens.astype(x_prompt.dtype)[None], (b, N_META, d))
    xp = jnp.concatenate([meta, x_prompt, jnp.zeros((b, tp - t_real, d), x_prompt.dtype)], axis=1)
    tab_p = _rope_tables(jnp.arange(tp))
    topk_p = min(TOPK_MAX, s // 4)

    n_seq, t_new, _ = x_sample.shape
    n_rows = n_seq * t_new
    past = page_table.shape[1] * cache_mla.shape[2]
    xs = x_sample.reshape(1, n_rows, d)
    tab_s = _rope_tables(past + jnp.arange(n_rows) % t_new)
    topk_s = min(TOPK_MAX, (past + t_new) // 4)
    mla_t = jnp.swapaxes(cache_mla, 2, 3)
    dkv_t = jnp.transpose(cache_dsa_kv, (0, 1, 3, 4, 5, 2)).reshape(cache_dsa_kv.shape[:2] + (256, -1))
    idx_t = jnp.swapaxes(cache_dsa_idx, 2, 3)
    fkv_t = jnp.transpose(cache_fox_kv, (0, 1, 3, 4, 5, 2)).reshape(cache_fox_kv.shape[:2] + (1024, -1))
    flf_t = jnp.swapaxes(cache_fox_logf, 2, 3)

    mla_p, dkv_p, idx_p, fkv_p, lf_p = [], [], [], [], []
    mla_s, dkv_s, idx_s, fkv_s, lf_s = [], [], [], [], []
    for l in range(depth):
        li = l // 2
        wg, wu, wd = w_ffn_gate[l].astype(BF16), w_ffn_up[l].astype(BF16), w_ffn_down[l].astype(BF16)
        if l % 2 == 0:
            ew = _even_weights(w_in_even[li], g_mla_q_lat[li], w_mla_uq[li], g_mla_kv_lat[li], w_mla_uk[li],
                               w_mla_uv[li], g_mla_q[li], g_mla_k[li], g_dsa_q[li], g_dsa_k[li], g_idx_k[li])
            w_out = w_out_even[li].astype(BF16)
            w_outs = (w_out[:512], w_out[512:])
            om, od, lat, kvd, ik = _prompt_even(xp, t_real, tab_p, g_mix[l], ew, topk_p)
            mla_p.append(lat)
            dkv_p.append(kvd.reshape(b, t_real, 2, DSA_KV_HEADS, HEAD_DIM))
            idx_p.append(ik)
            xp = _mix_ffn(xp, (om, od), w_outs, g_ffn[l], wg, wu, wd)

            qm, _, _, lat, qd, kvd, _, iq, misc = _even_project(xs, n_rows, tab_s, g_mix[l], ew)
            wukt = jnp.transpose(w_mla_uk[li], (1, 2, 0)).reshape(MLA_HEADS * MLA_NOPE, MLA_KV_LORA).astype(BF16)
            om = _mla_sample(page_table, mla_t, li, qm[0], lat[0], wukt, ew[5], n_seq)
            od = _dsa_sample(page_table, idx_t, dkv_t, li, qd[0], iq[0], misc[0], kvd[0], topk_s, n_seq)
            mla_s.append(lat.reshape(n_seq, t_new, MLA_LAT))
            dkv_s.append(kvd.reshape(n_seq, t_new, 2, DSA_KV_HEADS, HEAD_DIM))
            idx_s.append(misc[0, :, 0:IDX_DIM].reshape(n_seq, t_new, IDX_DIM))
            xs = _mix_ffn(xs, (om[None], od[None]), w_outs, g_ffn[l], wg, wu, wd)
        else:
            ow = _odd_weights(w_in_odd[li], b_fox_f[li], g_fox_q[li], g_fox_k[li])
            w_outs = (w_out_odd[li].astype(BF16),)
            of, kv, lf = _prompt_odd(xp, t_real, g_mix[l], ow)
            fkv_p.append(kv.reshape(b, t_real, 2, FOX_KV_HEADS, HEAD_DIM))
            lf_p.append(lf)
            xp = _mix_ffn(xp, (of,), w_outs, g_ffn[l], wg, wu, wd)

            q, kv, _, lf = _odd_project(xs, n_rows, g_mix[l], ow)
            of = _fox_sample(page_table, fkv_t, flf_t, li, q[0], kv[0], lf[0], n_seq)
            fkv_s.append(kv.reshape(n_seq, t_new, 2, FOX_KV_HEADS, HEAD_DIM))
            lf_s.append(lf.reshape(n_seq, t_new, FOX_HEADS))
            xs = _mix_ffn(xs, (of[None],), w_outs, g_ffn[l], wg, wu, wd)
    y_prompt = xp[:, N_META:t_real]
    y_sample = xs.reshape(n_seq, t_new, d)
    return (y_prompt, y_sample, jnp.stack(mla_p), jnp.stack(mla_s), jnp.stack(dkv_p), jnp.stack(dkv_s),
            jnp.stack(idx_p), jnp.stack(idx_s), jnp.stack(fkv_p), jnp.stack(fkv_s), jnp.stack(lf_p), jnp.stack(lf_s))
```

```python
import functools

import numpy as np
import jax
import jax.numpy as jnp
from jax import lax
from jax.experimental import pallas as pl
from jax.experimental.pallas import tpu as pltpu

F32 = jnp.float32
BF16 = jnp.bfloat16

N_META = 16
HEAD_DIM = 64
ROPE_THETA = 10000.0
NORM_EPS = 1e-6
MLA_HEADS = 8
MLA_NOPE = 64
MLA_ROPE = 32
MLA_QK = MLA_NOPE + MLA_ROPE
MLA_V = 64
MLA_Q_LORA = 384
MLA_KV_LORA = 256
MLA_LAT = MLA_KV_LORA + MLA_ROPE
DSA_HEADS = 8
DSA_KV_HEADS = 2
DSA_GROUP = DSA_HEADS // DSA_KV_HEADS
IDX_HEADS = 8
IDX_DIM = 64
TOPK_MAX = 256
IDX_SCALE = (IDX_DIM * IDX_HEADS) ** -0.5
FOX_HEADS = 16
FOX_KV_HEADS = 8

LANES = 128
VMEM_LIMIT = 56 * 1024 * 1024
NEG = -1e30
LOG2E = 1.4426950408889634
INT_MIN = -(2 ** 31)

P_QL, P_CKV, P_DQ, P_DK, P_DV, P_IQ, P_MISC, P_END = 0, 384, 640, 1152, 1280, 1408, 1920, 2048
M_KPE, M_IW = 64, 96


def _pick_tile(n):
    for t in (384, 256, 128):
        if n % t == 0:
            return t
    raise ValueError(f"row count {n} is not a multiple of 128")


def _cparams(sem):
    return pltpu.CompilerParams(dimension_semantics=sem, vmem_limit_bytes=VMEM_LIMIT)


def _lane(shape):
    return lax.broadcasted_iota(jnp.int32, shape, len(shape) - 1)


def _partner(x, d):
    lane = _lane(x.shape)
    return jnp.where((lane & d) != 0, pltpu.roll(x, d, 1), pltpu.roll(x, LANES - d, 1))


def _seg_ones(seg):
    r = lax.broadcasted_iota(jnp.int32, (LANES, LANES), 0) // seg
    c = lax.broadcasted_iota(jnp.int32, (LANES, LANES), 1) // seg
    return (r == c).astype(BF16)


def _seg_sumsq(x, ones_blk):
    sq = x * x
    hi = sq.astype(BF16)
    lo = (sq - hi.astype(F32)).astype(BF16)
    return _dot(hi, ones_blk) + _dot(lo, ones_blk)


def _swap_halves(x):
    return pltpu.roll(x, 64, 1)


def _rms_rows(x, g):
    ss = jnp.sum(x * x, axis=-1, keepdims=True)
    return x * lax.rsqrt(ss / x.shape[-1] + NORM_EPS) * g


def _dot(a, b):
    return jnp.dot(a, b, preferred_element_type=F32)


def _dot_nt(a, b):
    return lax.dot_general(a, b, (((1,), (1,)), ((), ())), preferred_element_type=F32)


def _even_proj_kernel(x_ref, tab_ref, gmix_ref, win_ref, gql_ref, wuq_ref, gkv_ref, wuk_ref, wuv_ref,
                      gq_ref, gdq_ref, gdk_ref, gmisc_ref,
                      qm_ref, km_ref, vm_ref, lat_ref, qd_ref, kvd_ref, kdb_ref, iq_ref, misc_ref):
    x = x_ref[0]
    tm = x.shape[0]
    h = _rms_rows(x, gmix_ref[...]).astype(BF16)
    p = _dot(h, win_ref[...])
    lane = _lane((tm, LANES))
    lo = lane < 64
    cos64, sin64 = tab_ref[:, 0:128], tab_ref[:, 128:256]
    cosq, sinq = tab_ref[:, 256:384], tab_ref[:, 384:512]
    cosm, sinm = tab_ref[:, 512:640], tab_ref[:, 640:768]

    ql = _rms_rows(p[:, P_QL:P_CKV], gql_ref[...]).astype(BF16)
    qm = _dot(ql, wuq_ref[...])
    gq = gq_ref[...]
    for hh in range(MLA_HEADS):
        c = qm[:, LANES * hh:LANES * (hh + 1)]
        c = c * cosq + _partner(c, 16) * sinq
        ss = jnp.sum(c * c, axis=-1, keepdims=True)
        c = c * lax.rsqrt(ss / MLA_QK + NORM_EPS) * gq
        qm_ref[0, :, LANES * hh:LANES * (hh + 1)] = c.astype(BF16)

    ckv = _rms_rows(p[:, P_CKV:P_DQ], gkv_ref[...])
    lat_ref[0, :, 0:MLA_KV_LORA] = ckv

    m = p[:, P_MISC:P_END]
    ikss = jnp.sum(jnp.where(lo, m * m, 0.0), axis=-1, keepdims=True)
    m = m * jnp.where(lo, lax.rsqrt(ikss / IDX_DIM + NORM_EPS) * gmisc_ref[...], 1.0)
    rot = jnp.where(lo, _partner(m, 32), _partner(m, 16))
    m = m * cosm + rot * sinm
    misc_ref[0] = m
    lat_ref[0, :, MLA_KV_LORA:MLA_LAT] = m[:, M_KPE:M_KPE + MLA_ROPE]

    ckv_b = ckv.astype(BF16)
    kn = _dot(ckv_b, wuk_ref[...])
    kpe = jnp.where((lane >= M_KPE) & (lane < M_KPE + MLA_ROPE), m, 0.0)
    for hh in range(MLA_HEADS):
        c = kn[:, LANES * hh:LANES * (hh + 1)] + kpe
        ss = jnp.sum(c * c, axis=-1, keepdims=True)
        km_ref[0, :, LANES * hh:LANES * (hh + 1)] = (c * lax.rsqrt(ss / MLA_QK + NORM_EPS)).astype(BF16)
    vm_ref[0] = _dot(ckv_b, wuv_ref[...]).astype(BF16)

    gdq = gdq_ref[...]
    ones64 = _seg_ones(HEAD_DIM)
    for j in range(DSA_HEADS // 2):
        c = p[:, P_DQ + LANES * j:P_DQ + LANES * (j + 1)]
        c = c * lax.rsqrt(_seg_sumsq(c, ones64) / HEAD_DIM + NORM_EPS) * gdq
        c = c * cos64 + _partner(c, 32) * sin64
        sw = _swap_halves(c)
        if (2 * j) // DSA_GROUP == 0:
            a, b = jnp.where(lo, c, 0.0), jnp.where(lo, sw, 0.0)
        else:
            a, b = jnp.where(lo, 0.0, sw), jnp.where(lo, 0.0, c)
        qd_ref[0, :, LANES * (2 * j):LANES * (2 * j + 1)] = a.astype(BF16)
        qd_ref[0, :, LANES * (2 * j + 1):LANES * (2 * j + 2)] = b.astype(BF16)

    c = p[:, P_DK:P_DV]
    c = c * lax.rsqrt(_seg_sumsq(c, ones64) / HEAD_DIM + NORM_EPS) * gdk_ref[...]
    kd = c * cos64 + _partner(c, 32) * sin64
    vd = p[:, P_DV:P_IQ]
    kvd_ref[0, :, 0:128] = kd
    kvd_ref[0, :, 128:256] = vd
    kdb_ref[0, :, 0:128] = kd.astype(BF16)
    kdb_ref[0, :, 128:256] = vd.astype(BF16)
    kdb_ref[0, :, 256:384] = m.astype(BF16)

    for j in range(IDX_HEADS // 2):
        c = p[:, P_IQ + LANES * j:P_IQ + LANES * (j + 1)]
        c = c * cos64 + _partner(c, 32) * sin64
        iq_ref[0, :, LANES * (2 * j):LANES * (2 * j + 1)] = jnp.where(lo, c, 0.0).astype(BF16)
        iq_ref[0, :, LANES * (2 * j + 1):LANES * (2 * j + 2)] = jnp.where(lo, _swap_halves(c), 0.0).astype(BF16)


def _rope_tables(pos):
    pos = pos.astype(F32)[:, None]

    def cs(half):
        inv = ROPE_THETA ** (-jnp.arange(half, dtype=F32) / half)
        ang = pos * inv
        c, s = jnp.cos(ang), jnp.sin(ang)
        return jnp.concatenate([c, c], -1), jnp.concatenate([-s, s], -1)

    c64, s64 = cs(32)
    c32, s32 = cs(16)
    t = pos.shape[0]
    one, zero = jnp.ones((t, 32), F32), jnp.zeros((t, 32), F32)
    cos64, sin64 = jnp.concatenate([c64, c64], -1), jnp.concatenate([s64, s64], -1)
    cosq = jnp.concatenate([one, one, c32, one], -1)
    sinq = jnp.concatenate([zero, zero, s32, zero], -1)
    cosm = jnp.concatenate([c64, c32, one], -1)
    sinm = jnp.concatenate([s64, s32, zero], -1)
    return jnp.concatenate([cos64, sin64, cosq, sinq, cosm, sinm], -1)


def _even_weights(w_in, g_q_lat, w_uq, g_kv_lat, w_uk, w_uv, g_mq, g_mk, g_dq, g_dk, g_ik):
    d = w_in.shape[0]
    o_dsa = MLA_Q_LORA + MLA_KV_LORA + MLA_ROPE
    o_dq, o_dk, o_dv, o_iq = o_dsa, o_dsa + 512, o_dsa + 640, o_dsa + 768
    o_iw, o_ik = o_iq + 512, o_iq + 520
    win = jnp.concatenate([
        w_in[:, 0:640], w_in[:, o_dq:o_iw], w_in[:, o_ik:o_ik + 64], w_in[:, 640:672], w_in[:, o_iw:o_iw + 8],
        jnp.zeros((d, 24), w_in.dtype)], axis=1).astype(BF16)
    wuq = w_uq.reshape(MLA_Q_LORA, MLA_HEADS, MLA_QK)
    wuq = jnp.pad(wuq, ((0, 0), (0, 0), (0, LANES - MLA_QK))).reshape(MLA_Q_LORA, MLA_HEADS * LANES).astype(BF16)
    wuk = jnp.pad(w_uk, ((0, 0), (0, 0), (0, LANES - MLA_NOPE))).reshape(MLA_KV_LORA, MLA_HEADS * LANES).astype(BF16)
    wuv = w_uv.reshape(MLA_KV_LORA, MLA_HEADS * MLA_V).astype(BF16)
    gq = jnp.concatenate([g_mq * g_mk * (MLA_QK ** -0.5 * LOG2E), jnp.zeros((LANES - MLA_QK,), F32)])[None]
    gdq = (jnp.concatenate([g_dq, g_dq]) * (HEAD_DIM ** -0.5 * LOG2E))[None]
    gdk = jnp.concatenate([g_dk, g_dk])[None]
    gmisc = jnp.concatenate([g_ik, jnp.ones((64,), F32)])[None]
    return win, g_q_lat[None], wuq, g_kv_lat[None], wuk, wuv, gq, gdq, gdk, gmisc


def _even_project(x, t_out, tab, g_mix, ew):
    b, tp, d = x.shape
    tm = _pick_tile(tp)
    win, gql, wuq, gkv, wuk, wuv, gq, gdq, gdk, gmisc = ew
    row = lambda w: pl.BlockSpec((1, tm, w), lambda i, j: (i, j, 0))
    full = lambda a: pl.BlockSpec(a.shape, lambda i, j: (0,) * a.ndim)
    outs = [
        jax.ShapeDtypeStruct((b, tp, 1024), BF16), jax.ShapeDtypeStruct((b, tp, 1024), BF16),
        jax.ShapeDtypeStruct((b, tp, 512), BF16), jax.ShapeDtypeStruct((b, t_out, MLA_LAT), F32),
        jax.ShapeDtypeStruct((b, tp, 1024), BF16), jax.ShapeDtypeStruct((b, t_out, 256), F32),
        jax.ShapeDtypeStruct((b, tp, 384), BF16), jax.ShapeDtypeStruct((b, tp, 1024), BF16),
        jax.ShapeDtypeStruct((b, tp, LANES), F32),
    ]
    params = (g_mix[None], win, gql, wuq, gkv, wuk, wuv, gq, gdq, gdk, gmisc)
    return pl.pallas_call(
        _even_proj_kernel,
        grid=(b, tp // tm),
        in_specs=[row(d), pl.BlockSpec((tm, 768), lambda i, j: (j, 0))] + [full(a) for a in params],
        out_specs=[row(1024), row(1024), row(512), row(MLA_LAT), row(1024), row(256), row(384), row(1024), row(LANES)],
        out_shape=outs,
        compiler_params=_cparams(("parallel", "parallel")),
        name="even_project",
    )(x, tab, *params)


def _flash_init(m_ref, l_ref, acc_ref):
    m_ref[...] = jnp.full(m_ref.shape, NEG, F32)
    l_ref[...] = jnp.zeros(l_ref.shape, F32)
    acc_ref[...] = jnp.zeros(acc_ref.shape, F32)


def _flash_step(s, v, m_ref, l_ref, acc_ref, nt=False):
    tk = s.shape[1]
    tiles = [s[:, LANES * j:LANES * (j + 1)] for j in range(tk // LANES)]
    mx = functools.reduce(jnp.maximum, tiles)
    m_prev = m_ref[...]
    m_new = jnp.maximum(m_prev, jnp.max(mx, axis=-1, keepdims=True))
    alpha = jnp.exp2(m_prev - m_new)
    ps = [jnp.exp2(tl - m_new) for tl in tiles]
    l_ref[...] = alpha * l_ref[...] + functools.reduce(jnp.add, ps)
    p = jnp.concatenate(ps, axis=1).astype(BF16)
    pv = _dot_nt(p, v) if nt else _dot(p, v)
    n = pv.shape[1]
    a = alpha if n == LANES else jnp.concatenate([alpha] * (n // LANES), axis=1)
    acc_ref[...] = a * acc_ref[...] + pv
    m_ref[...] = m_new


def _flash_out(l_ref, acc_ref):
    return acc_ref[...] / jnp.sum(l_ref[...], axis=-1, keepdims=True)


def _causal_tile_mask(rows, t):
    r = lax.broadcasted_iota(jnp.int32, (rows, t), 0) % t
    return r >= lax.broadcasted_iota(jnp.int32, (rows, t), 1)


def _mla_prompt_kernel(q_ref, k_ref, v_ref, o_ref, m_ref, l_ref, acc_ref, *, t):
    qi = pl.program_id(1)
    diag = _causal_tile_mask(2 * t, t)
    lo = _lane((t, LANES)) < 64
    _flash_init(m_ref, l_ref, acc_ref)

    def step(j, masked):
        rows = pl.ds(pl.multiple_of(j * t, t), t)
        for pair in range(MLA_HEADS // 2):
            s = jnp.concatenate([_dot_nt(q_ref[0, :, LANES * hh:LANES * (hh + 1)],
                                         k_ref[0, rows, LANES * hh:LANES * (hh + 1)])
                                 for hh in (2 * pair, 2 * pair + 1)], axis=0)
            if masked:
                s = jnp.where(diag, s, NEG)
            _flash_step(s, v_ref[0, rows, LANES * pair:LANES * (pair + 1)],
                        m_ref.at[pair], l_ref.at[pair], acc_ref.at[pair])

    lax.fori_loop(0, qi, lambda j, c: (step(j, False), c)[1], 0)
    step(qi, True)
    for pair in range(MLA_HEADS // 2):
        out = _flash_out(l_ref.at[pair], acc_ref.at[pair])
        o_ref[0, :, LANES * pair:LANES * (pair + 1)] = jnp.where(lo, out[0:t], out[t:2 * t]).astype(o_ref.dtype)


def _mla_prompt(qm, km, vm):
    b, tp, _ = qm.shape
    t = _pick_tile(tp)
    return pl.pallas_call(
        functools.partial(_mla_prompt_kernel, t=t),
        grid=(b, tp // t),
        in_specs=[pl.BlockSpec((1, t, 1024), lambda i, j: (i, j, 0)),
                  pl.BlockSpec((1, tp, 1024), lambda i, j: (i, 0, 0)),
                  pl.BlockSpec((1, tp, 512), lambda i, j: (i, 0, 0))],
        out_specs=pl.BlockSpec((1, t, 512), lambda i, j: (i, j, 0)),
        out_shape=jax.ShapeDtypeStruct((b, tp, 512), BF16),
        scratch_shapes=[pltpu.VMEM((MLA_HEADS // 2, 2 * t, LANES), F32)] * 3,
        compiler_params=_cparams(("parallel", "arbitrary")),
        name="mla_prompt_attention",
    )(qm, km, vm)


def _sort_key(score):
    bits = pltpu.bitcast(score + 0.0, jnp.int32)
    return jnp.where(bits < 0, bits ^ 0x7FFFFFFF, bits)


def _row_count(pred):
    return jnp.sum(pred.astype(F32), axis=-1, keepdims=True)


def _topk_threshold(count, rows, k, idx_bits):
    kf = jnp.float32(k)
    few = count(lambda key, idx, sl: key > INT_MIN) <= kf
    c0 = count(lambda key, idx, sl: key >= 0)
    ok0 = c0 >= kf
    t0 = jnp.where(ok0, 0, INT_MIN).astype(jnp.int32)
    n0 = jnp.where(ok0, c0, jnp.float32(2.0 ** 30))

    def settled(n_ge):
        return jnp.min(jnp.where((n_ge == kf) | few, 1.0, 0.0)) > 0.0

    def key_bit(state):
        i, t, n_ge = state
        cand = t | (jnp.int32(1) << (30 - i))
        c = count(lambda key, idx, sl: key >= cand[sl])
        ok = c >= kf
        return i + 1, jnp.where(ok, cand, t), jnp.where(ok, c, n_ge)

    _, thr, n_ge = lax.while_loop(lambda st: (st[0] < 31) & jnp.logical_not(settled(st[2])), key_bit,
                                  (jnp.int32(0), t0, n0))
    all_idx = jnp.full((rows, 1), (1 << idx_bits) - 1, jnp.int32)

    def tie_bound():
        need = kf - count(lambda key, idx, sl: key > thr[sl])

        def idx_bit(i, hi):
            cand = hi & ~(jnp.int32(1) << (idx_bits - 1 - i))
            ok = count(lambda key, idx, sl: (key == thr[sl]) & (idx <= cand[sl])) >= need
            return jnp.where(ok, cand, hi)

        return lax.fori_loop(0, idx_bits, idx_bit, all_idx)

    has_tie = jnp.max(jnp.where((n_ge > kf) & (thr > INT_MIN) & jnp.logical_not(few), 1.0, 0.0)) > 0.0
    return thr, lax.cond(has_tie, tie_bound, lambda: all_idx)


def _selected(key, idx, thr, bound):
    return ((key > thr) | ((key == thr) & (idx <= bound))) & (key > INT_MIN)


def _dsa_prompt_kernel(q_ref, iq_ref, misc_ref, kdb_ref, o_ref, key_ref, bias_ref, m_ref, l_ref, acc_ref,
                       *, t, topk, idx_bits):
    qi = pl.program_id(1)
    nk = qi + 1
    row_pos = qi * t + lax.broadcasted_iota(jnp.int32, (t, t), 0)
    col_iota = lax.broadcasted_iota(jnp.int32, (t, t), 1)
    iw = misc_ref[0][:, M_IW:M_IW + IDX_HEADS] * IDX_SCALE

    def score_chunk(c, carry):
        rows = pl.ds(pl.multiple_of(c * t, t), t)
        ik = kdb_ref[0, rows, 256:384]
        acc = jnp.zeros((t, t), F32)
        for hh in range(IDX_HEADS):
            d = _dot_nt(iq_ref[0, :, LANES * hh:LANES * (hh + 1)], ik)
            acc = acc + iw[:, hh:hh + 1] * jnp.maximum(d, 0.0)
        ok = row_pos >= c * t + col_iota
        key_ref[c] = jnp.where(ok, _sort_key(acc), INT_MIN)
        return carry

    lax.fori_loop(0, nk, score_chunk, 0)

    blk_iota = lax.broadcasted_iota(jnp.int32, (LANES, LANES), 1)

    def count(f):
        out = []
        for rb in range(t // LANES):
            sl = slice(LANES * rb, LANES * (rb + 1))

            def body(c, acc, sl=sl):
                for j in range(t // LANES):
                    key = key_ref[c, sl, LANES * j:LANES * (j + 1)]
                    acc = acc + f(key, c * t + LANES * j + blk_iota, sl).astype(F32)
                return acc

            acc = lax.fori_loop(0, nk, body, jnp.zeros((LANES, LANES), F32))
            out.append(jnp.sum(acc, axis=-1, keepdims=True))
        return jnp.concatenate(out, axis=0)

    thr, bound = _topk_threshold(count, t, topk, idx_bits)

    _flash_init(m_ref, l_ref, acc_ref)

    def attend_chunk(c, carry):
        rows = pl.ds(pl.multiple_of(c * t, t), t)
        bias_ref[...] = jnp.where(_selected(key_ref[c], c * t + col_iota, thr, bound), 0.0, NEG)
        kc = kdb_ref[0, rows, 0:128]
        vc = kdb_ref[0, rows, 128:256]
        for g in range(DSA_KV_HEADS):
            s = jnp.concatenate([_dot_nt(q_ref[0, :, LANES * hh:LANES * (hh + 1)], kc) + bias_ref[...]
                                 for hh in range(DSA_GROUP * g, DSA_GROUP * (g + 1))], axis=0)
            _flash_step(s, vc, m_ref.at[g], l_ref.at[g], acc_ref.at[g])
        return carry

    lax.fori_loop(0, nk, attend_chunk, 0)

    lo = _lane((t, LANES)) < 64
    for g in range(DSA_KV_HEADS):
        out = _flash_out(l_ref.at[g], acc_ref.at[g])
        for i in range(DSA_GROUP // 2):
            a, b = out[2 * i * t:(2 * i + 1) * t], out[(2 * i + 1) * t:(2 * i + 2) * t]
            pair = jnp.where(lo, a, _swap_halves(b)) if g == 0 else jnp.where(lo, _swap_halves(a), b)
            col = LANES * (g * DSA_GROUP // 2 + i)
            o_ref[0, :, col:col + LANES] = pair.astype(o_ref.dtype)


def _dsa_prompt(qd, iq, misc, kdb, topk):
    b, tp, _ = qd.shape
    t = _pick_tile(tp)
    n = tp // t
    idx_bits = max(1, int(tp - 1).bit_length())
    return pl.pallas_call(
        functools.partial(_dsa_prompt_kernel, t=t, topk=topk, idx_bits=idx_bits),
        grid=(b, n),
        in_specs=[pl.BlockSpec((1, t, 1024), lambda i, j: (i, j, 0)),
                  pl.BlockSpec((1, t, 1024), lambda i, j: (i, j, 0)),
                  pl.BlockSpec((1, t, LANES), lambda i, j: (i, j, 0)),
                  pl.BlockSpec((1, tp, 384), lambda i, j: (i, 0, 0))],
        out_specs=pl.BlockSpec((1, t, 512), lambda i, j: (i, j, 0)),
        out_shape=jax.ShapeDtypeStruct((b, tp, 512), BF16),
        scratch_shapes=[pltpu.VMEM((n, t, t), jnp.int32), pltpu.VMEM((t, t), F32)]
        + [pltpu.VMEM((DSA_KV_HEADS, DSA_GROUP * t, LANES), F32)] * 3,
        compiler_params=_cparams(("parallel", "arbitrary")),
        name="dsa_prompt_attention",
    )(qd, iq, misc, kdb)


def _mix_ffn_kernel(*refs, n_mix):
    x_ref = refs[0]
    o_refs = refs[1:1 + n_mix]
    w_refs = refs[1 + n_mix:1 + 2 * n_mix]
    g_ref, wg_ref, wu_ref, wd_ref, y_ref = refs[1 + 2 * n_mix:]
    x = x_ref[0]
    for o_ref, w_ref in zip(o_refs, w_refs):
        x = x + _dot(o_ref[0], w_ref[...])
    h = _rms_rows(x, g_ref[...]).astype(BF16)
    gate = _dot(h, wg_ref[...])
    up = _dot(h, wu_ref[...])
    act = (gate * jax.nn.sigmoid(gate) * up).astype(BF16)
    y_ref[0] = x + _dot(act, wd_ref[...])


def _mix_ffn(x, mixes, w_outs, g, wg, wu, wd):
    b, tp, d = x.shape
    tm = _pick_tile(tp)
    n_mix = len(mixes)
    row = lambda w: pl.BlockSpec((1, tm, w), lambda i, j: (i, j, 0))
    full = lambda a: pl.BlockSpec(a.shape, lambda i, j: (0,) * a.ndim)
    params = (g[None], wg, wu, wd)
    return pl.pallas_call(
        functools.partial(_mix_ffn_kernel, n_mix=n_mix),
        grid=(b, tp // tm),
        in_specs=[row(d)] + [row(o.shape[-1]) for o in mixes] + [full(w) for w in w_outs] + [full(a) for a in params],
        out_specs=row(d),
        out_shape=jax.ShapeDtypeStruct((b, tp, d), F32),
        compiler_params=_cparams(("parallel", "parallel")),
        name="mix_ffn",
    )(x, *mixes, *w_outs, *params)


O_Q, O_K, O_V, O_F, O_END = 0, 1024, 1536, 2048, 2176


def _split3(x):
    a = x.astype(BF16)
    r = x - a.astype(F32)
    b = r.astype(BF16)
    c = (r - b.astype(F32)).astype(BF16)
    return a, b, c


FQ_W, FK_W, FV_OFF, FKV_W = 256, 256, 1024, 1536


def _odd_proj_kernel(x_ref, gmix_ref, win_ref, bf_ref, gq_ref, gk_ref,
                     q_ref, kv_ref, kvb_ref, lf_ref, carry_ref):
    @pl.when(pl.program_id(1) == 0)
    def _():
        carry_ref[...] = jnp.zeros(carry_ref.shape, F32)

    x = x_ref[0]
    tm = x.shape[0]
    h = _rms_rows(x, gmix_ref[...]).astype(BF16)
    p = _dot(h, win_ref[...])
    lane = _lane((tm, LANES))
    lo = lane < 64

    lf = jax.nn.log_sigmoid(p[:, O_F:O_END] + bf_ref[...])
    lf_ref[0] = lf[:, 0:FOX_HEADS]
    tri = (lax.broadcasted_iota(jnp.int32, (tm, tm), 0) >= lax.broadcasted_iota(jnp.int32, (tm, tm), 1)).astype(BF16)
    cum = carry_ref[0:1, :]
    for part in _split3(lf):
        cum = cum + _dot(tri, part)
    carry_ref[0:1, :] = cum[tm - 1:tm, :]
    heads = lane < FOX_HEADS
    c2 = cum * LOG2E
    hi = jnp.where(heads, c2.astype(BF16).astype(F32), 0.0)
    r1 = c2 - hi
    mid = jnp.where(heads, r1.astype(BF16).astype(F32), 0.0)
    low = jnp.where(heads, (r1 - mid).astype(BF16).astype(F32), 0.0)
    packed = hi + pltpu.roll(mid, 16, 1) + pltpu.roll(low, 32, 1)
    bias_key = (packed + jnp.where((lane >= 64) & (lane < 112), 1.0, 0.0)).astype(BF16)
    cq = _swap_halves(packed)

    gq, gk = gq_ref[...], gk_ref[...]
    ones64 = _seg_ones(HEAD_DIM)
    for j in range(FOX_HEADS // 2):
        c = p[:, O_Q + LANES * j:O_Q + LANES * (j + 1)]
        c = c * lax.rsqrt(_seg_sumsq(c, ones64) / HEAD_DIM + NORM_EPS) * gq
        sw = _swap_halves(c)
        if j % 2 == 0:
            pair = (jnp.where(lo, c, 0.0), jnp.where(lo, sw, 0.0))
        else:
            pair = (jnp.where(lo, 0.0, sw), jnp.where(lo, 0.0, c))
        for i, qh in enumerate(pair):
            hh = 2 * j + i
            minus = (lane == hh) | (lane == 16 + hh) | (lane == 32 + hh)
            own = (lane == 64 + hh) | (lane == 80 + hh) | (lane == 96 + hh)
            q_ref[0, :, FQ_W * hh:FQ_W * hh + LANES] = qh.astype(BF16)
            q_ref[0, :, FQ_W * hh + LANES:FQ_W * (hh + 1)] = jnp.where(minus, -1.0, jnp.where(own, cq, 0.0)).astype(BF16)
    for j in range(FOX_KV_HEADS // 2):
        c = p[:, O_K + LANES * j:O_K + LANES * (j + 1)]
        c = c * lax.rsqrt(_seg_sumsq(c, ones64) / HEAD_DIM + NORM_EPS) * gk
        kv_ref[0, :, LANES * j:LANES * (j + 1)] = c
        kvb_ref[0, :, FK_W * j:FK_W * j + LANES] = c.astype(BF16)
        kvb_ref[0, :, FK_W * j + LANES:FK_W * (j + 1)] = bias_key
    v = p[:, O_V:O_F]
    kv_ref[0, :, 512:1024] = v
    kvb_ref[0, :, FV_OFF:FKV_W] = v.astype(BF16)


def _odd_weights(w_in, b_f, g_q, g_k):
    d = w_in.shape[0]
    win = jnp.concatenate([w_in, jnp.zeros((d, O_END - w_in.shape[1]), w_in.dtype)], axis=1).astype(BF16)
    bf = jnp.concatenate([b_f, jnp.zeros((LANES - FOX_HEADS,), F32)])[None]
    gq = (jnp.concatenate([g_q, g_q]) * (HEAD_DIM ** -0.5 * LOG2E))[None]
    gk = jnp.concatenate([g_k, g_k])[None]
    return win, bf, gq, gk


def _odd_project(x, t_out, g_mix, ow):
    b, tp, d = x.shape
    tm = _pick_tile(tp)
    n = tp // tm
    win, bf, gq, gk = ow
    row = lambda w: pl.BlockSpec((1, tm, w), lambda i, j: (i, j, 0))
    full = lambda a: pl.BlockSpec(a.shape, lambda i, j: (0,) * a.ndim)
    params = (g_mix[None], win, bf, gq, gk)
    outs = [
        jax.ShapeDtypeStruct((b, tp, FOX_HEADS * FQ_W), BF16), jax.ShapeDtypeStruct((b, t_out, 1024), F32),
        jax.ShapeDtypeStruct((b, tp, FKV_W), BF16), jax.ShapeDtypeStruct((b, t_out, FOX_HEADS), F32),
    ]
    return pl.pallas_call(
        _odd_proj_kernel,
        grid=(b, n),
        in_specs=[row(d)] + [full(a) for a in params],
        out_specs=[row(FOX_HEADS * FQ_W), row(1024), row(FKV_W), row(FOX_HEADS)],
        out_shape=outs,
        scratch_shapes=[pltpu.VMEM((8, LANES), F32)],
        compiler_params=_cparams(("parallel", "arbitrary")),
        name="odd_project",
    )(x, *params)


def _fox_prompt_kernel(q_ref, kv_ref, o_ref, m_ref, l_ref, acc_ref, *, t):
    qi = pl.program_id(1)
    diag = _causal_tile_mask(2 * t, t)
    lo = _lane((t, LANES)) < 64
    _flash_init(m_ref, l_ref, acc_ref)

    def step(j, masked):
        rows = pl.ds(pl.multiple_of(j * t, t), t)
        for kvh in range(FOX_KV_HEADS):
            grp = kvh // 2
            k = kv_ref[0, rows, FK_W * grp:FK_W * (grp + 1)]
            s = jnp.concatenate([_dot_nt(q_ref[0, :, FQ_W * hh:FQ_W * (hh + 1)], k) for hh in (2 * kvh, 2 * kvh + 1)],
                                axis=0)
            if masked:
                s = jnp.where(diag, s, NEG)
            _flash_step(s, kv_ref[0, rows, FV_OFF + LANES * grp:FV_OFF + LANES * (grp + 1)],
                        m_ref.at[kvh], l_ref.at[kvh], acc_ref.at[kvh])

    lax.fori_loop(0, qi, lambda j, c: (step(j, False), c)[1], 0)
    step(qi, True)
    for kvh in range(FOX_KV_HEADS):
        out = _flash_out(l_ref.at[kvh], acc_ref.at[kvh])
        a, b = out[0:t], out[t:2 * t]
        pair = jnp.where(lo, a, _swap_halves(b)) if kvh % 2 == 0 else jnp.where(lo, _swap_halves(a), b)
        o_ref[0, :, LANES * kvh:LANES * (kvh + 1)] = pair.astype(o_ref.dtype)


def _fox_prompt(q, kvb):
    b, tp, _ = q.shape
    t = _pick_tile(tp)
    return pl.pallas_call(
        functools.partial(_fox_prompt_kernel, t=t),
        grid=(b, tp // t),
        in_specs=[pl.BlockSpec((1, t, FOX_HEADS * FQ_W), lambda i, j: (i, j, 0)),
                  pl.BlockSpec((1, tp, FKV_W), lambda i, j: (i, 0, 0))],
        out_specs=pl.BlockSpec((1, t, 1024), lambda i, j: (i, j, 0)),
        out_shape=jax.ShapeDtypeStruct((b, tp, 1024), BF16),
        scratch_shapes=[pltpu.VMEM((FOX_KV_HEADS, 2 * t, LANES), F32)] * 3,
        compiler_params=_cparams(("parallel", "arbitrary")),
        name="fox_prompt_attention",
    )(q, kvb)


def _page_dma(cache_ref, li, pt_ref, seq, page0, n_sub, ppc, buf_ref, slot, sem, wait):
    width = cache_ref.shape[-1]

    def body(sub, carry):
        for pp in range(ppc):
            page = pt_ref[seq, page0 + sub * ppc + pp]
            cp = pltpu.make_async_copy(cache_ref.at[li, page],
                                       buf_ref.at[slot, sub, :, pl.ds(pp * width, width)], sem.at[slot])
            if wait:
                cp.wait()
            else:
                cp.start()
        return carry

    lax.fori_loop(0, n_sub, body, 0)


def _tail_mask(rows, heads):
    r = lax.broadcasted_iota(jnp.int32, (rows, LANES), 0)
    return _lane((rows, LANES)) <= r // heads


def _feature_major_tail(x, n_seq):
    t_new = x.shape[0] // n_seq
    xt = jnp.swapaxes(x.reshape(n_seq, t_new, x.shape[1]), 1, 2)
    return jnp.pad(xt, ((0, 0), (0, 0), (0, LANES - t_new)))


def _mla_sample_kernel(pt_ref, cache_ref, qn_ref, qr_ref, tail_ref, wukt_ref, wuv_ref, o_ref,
                       buf, sem, m_ref, l_ref, acc_ref, *, li, n_sub, ppc, t_new):
    s = pl.program_id(0)
    slot = s % 2
    dma = functools.partial(_page_dma, cache_ref, li, pt_ref, page0=0, n_sub=n_sub, ppc=ppc, buf_ref=buf, sem=sem)

    @pl.when(s == 0)
    def _():
        dma(seq=s, slot=slot, wait=False)

    @pl.when(s + 1 < pl.num_programs(0))
    def _():
        dma(seq=s + 1, slot=1 - slot, wait=False)

    dma(seq=s, slot=slot, wait=True)

    rows = t_new * MLA_HEADS
    wukt = wukt_ref[...]
    qabs = _dot(qn_ref[0], wukt).astype(BF16)
    qr = qr_ref[0]
    _flash_init(m_ref, l_ref, acc_ref)

    def block(lat_t, mask):
        ckv = lat_t[0:MLA_KV_LORA].astype(BF16)
        kpe = lat_t[MLA_KV_LORA:MLA_LAT]
        kn = _dot(wukt, ckv)
        tk = kn.shape[-1]
        ss = jnp.sum((kn * kn).reshape(MLA_HEADS, MLA_NOPE, tk), axis=1)
        ss = ss + jnp.sum(kpe * kpe, axis=0, keepdims=True)
        rs = lax.rsqrt(ss / MLA_QK + NORM_EPS)
        sc = _dot(qabs, ckv) + _dot(qr, kpe.astype(BF16))
        sc = sc * jnp.concatenate([rs] * t_new, axis=0)
        if mask is not None:
            sc = jnp.where(mask, sc, NEG)
        _flash_step(sc, ckv, m_ref, l_ref, acc_ref, nt=True)

    def main(sub, carry):
        block(buf[slot, sub], None)
        return carry

    lax.fori_loop(0, n_sub, main, 0, unroll=2 if n_sub % 2 == 0 else 1)
    block(tail_ref[0], _tail_mask(rows, MLA_HEADS))

    o_lat = _flash_out(l_ref, acc_ref).astype(BF16)
    o = _dot(o_lat, wuv_ref[...])
    own = lax.broadcasted_iota(jnp.int32, (MLA_HEADS, 512), 1) // MLA_V == \
        lax.broadcasted_iota(jnp.int32, (MLA_HEADS, 512), 0)
    for q in range(t_new):
        blk = jnp.where(own, o[MLA_HEADS * q:MLA_HEADS * (q + 1)], 0.0)
        o_ref[0, q:q + 1, :] = jnp.sum(blk, axis=0, keepdims=True).astype(o_ref.dtype)


def _sub_pages(n_pages, want):
    ppc = want
    while n_pages % ppc:
        ppc //= 2
    return ppc


def _mla_sample(page_table, cache_t, li, qm, lat_new, wukt, wuv, n_seq):
    t_new = qm.shape[0] // n_seq
    rows = t_new * MLA_HEADS
    n_pages = page_table.shape[1]
    page = cache_t.shape[-1]
    ppc = _sub_pages(n_pages, 8)
    n_sub = n_pages // ppc
    q4 = qm.reshape(n_seq, t_new, MLA_HEADS, LANES)
    eye = jnp.eye(MLA_HEADS, dtype=qm.dtype)
    qn = (q4[:, :, :, None, :MLA_NOPE] * eye[None, None, :, :, None]).reshape(n_seq, rows, MLA_HEADS * MLA_NOPE)
    qr = q4[..., MLA_NOPE:MLA_QK].reshape(n_seq, rows, MLA_ROPE)
    tail = _feature_major_tail(lat_new, n_seq)
    seq3 = lambda a: pl.BlockSpec((1,) + a.shape[1:], lambda s, pt: (s, 0, 0))
    full = lambda a: pl.BlockSpec(a.shape, lambda s, pt: (0,) * a.ndim)
    out = pl.pallas_call(
        functools.partial(_mla_sample_kernel, li=li, n_sub=n_sub, ppc=ppc, t_new=t_new),
        grid_spec=pltpu.PrefetchScalarGridSpec(
            num_scalar_prefetch=1, grid=(n_seq,),
            in_specs=[pl.BlockSpec(memory_space=pl.ANY), seq3(qn), seq3(qr), seq3(tail), full(wukt), full(wuv)],
            out_specs=pl.BlockSpec((1, t_new, 512), lambda s, pt: (s, 0, 0)),
            scratch_shapes=[pltpu.VMEM((2, n_sub, MLA_LAT, ppc * page), F32), pltpu.SemaphoreType.DMA((2,)),
                            pltpu.VMEM((rows, LANES), F32), pltpu.VMEM((rows, LANES), F32),
                            pltpu.VMEM((rows, MLA_KV_LORA), F32)]),
        out_shape=jax.ShapeDtypeStruct((n_seq, t_new, 512), BF16),
        compiler_params=_cparams(("arbitrary",)),
        name="mla_sample_attention",
    )(page_table, cache_t, qn, qr, tail, wukt, wuv)
    return out.reshape(n_seq * t_new, 512)


def _dsa_sample_kernel(pt_ref, idx_cache, kv_cache, iq_ref, iw_ref, q_ref, itail_ref, ktail_ref, o_ref,
                       ibuf, kbuf, isem, ksem, key_ref, m_ref, l_ref, acc_ref,
                       *, li, n_sub, ppc, t_new, topk, idx_bits):
    s = pl.program_id(0)
    slot = s % 2
    idma = functools.partial(_page_dma, idx_cache, li, pt_ref, page0=0, n_sub=n_sub, ppc=ppc, buf_ref=ibuf, sem=isem)
    kdma = functools.partial(_page_dma, kv_cache, li, pt_ref, page0=0, n_sub=n_sub, ppc=ppc, buf_ref=kbuf, sem=ksem)

    @pl.when(s == 0)
    def _():
        idma(seq=s, slot=slot, wait=False)
        kdma(seq=s, slot=slot, wait=False)

    @pl.when(s + 1 < pl.num_programs(0))
    def _():
        idma(seq=s + 1, slot=1 - slot, wait=False)
        kdma(seq=s + 1, slot=1 - slot, wait=False)

    idma(seq=s, slot=slot, wait=True)

    rows = t_new * IDX_HEADS
    tk = ibuf.shape[-1]
    n_past = n_sub * tk
    iq = iq_ref[0]
    iw = iw_ref[0] * IDX_SCALE
    qrow = lax.broadcasted_iota(jnp.int32, (8, 1), 0)

    def keys_of(ik_t, admissible):
        w = jnp.maximum(_dot(iq, ik_t.astype(BF16)), 0.0) * iw
        width = w.shape[-1]
        sc = jnp.sum(w.reshape(t_new, IDX_HEADS, width), axis=1)
        sc = jnp.concatenate([sc, jnp.zeros((8 - t_new, width), F32)], axis=0)
        return jnp.where(admissible, _sort_key(sc), INT_MIN)

    for sub in range(n_sub):
        key_ref[:, sub * tk:(sub + 1) * tk] = keys_of(ibuf[slot, sub], qrow < t_new)
    tail_ok = (_lane((8, LANES)) <= qrow) & (qrow < t_new)
    key_ref[:, n_past:n_past + LANES] = keys_of(itail_ref[0], tail_ok)

    def count(f):
        return _row_count(f(key_ref[...], _lane(key_ref.shape), slice(None)))

    thr, bound = _topk_threshold(count, 8, topk, idx_bits)

    kdma(seq=s, slot=slot, wait=True)
    q = q_ref[0]
    _flash_init(m_ref, l_ref, acc_ref)

    def attend(lane0, width, kv_t):
        key = key_ref[:, lane0:lane0 + width]
        sel = _selected(key, lane0 + _lane((8, width)), thr, bound)
        bias = jnp.where(sel, 0.0, NEG)
        bias = jnp.concatenate([jnp.broadcast_to(bias[qq:qq + 1], (DSA_HEADS, width)) for qq in range(t_new)], axis=0)
        sc = _dot(q, kv_t[0:128].astype(BF16)) + bias
        _flash_step(sc, kv_t[128:256].astype(BF16), m_ref, l_ref, acc_ref, nt=True)

    for sub in range(n_sub):
        attend(sub * tk, tk, kbuf[slot, sub])
    attend(n_past, LANES, ktail_ref[0])
    o_ref[0] = _flash_out(l_ref, acc_ref)


def _dsa_sample(page_table, idx_t, kv_t, li, qd, iq, misc, kv_new, topk, n_seq):
    t_new = qd.shape[0] // n_seq
    rows = t_new * DSA_HEADS
    n_pages = page_table.shape[1]
    page = kv_t.shape[-1]
    ppc = _sub_pages(n_pages, 8)
    n_sub = n_pages // ppc
    tk = ppc * page
    idx_bits = int(n_pages * page + LANES - 1).bit_length()
    q = qd.reshape(n_seq, rows, LANES)
    iqr = iq.reshape(n_seq, rows, LANES)[..., :IDX_DIM]
    iw = misc[:, M_IW:M_IW + IDX_HEADS].reshape(n_seq, rows, 1)
    itail = _feature_major_tail(misc[:, 0:IDX_DIM], n_seq)
    ktail = _feature_major_tail(kv_new, n_seq)
    seq3 = lambda a: pl.BlockSpec((1,) + a.shape[1:], lambda s, pt: (s, 0, 0))
    out = pl.pallas_call(
        functools.partial(_dsa_sample_kernel, li=li, n_sub=n_sub, ppc=ppc, t_new=t_new, topk=topk, idx_bits=idx_bits),
        grid_spec=pltpu.PrefetchScalarGridSpec(
            num_scalar_prefetch=1, grid=(n_seq,),
            in_specs=[pl.BlockSpec(memory_space=pl.ANY), pl.BlockSpec(memory_space=pl.ANY),
                      seq3(iqr), seq3(iw), seq3(q), seq3(itail), seq3(ktail)],
            out_specs=pl.BlockSpec((1, rows, LANES), lambda s, pt: (s, 0, 0)),
            scratch_shapes=[pltpu.VMEM((2, n_sub, IDX_DIM, tk), F32), pltpu.VMEM((2, n_sub, 256, tk), F32),
                            pltpu.SemaphoreType.DMA((2,)), pltpu.SemaphoreType.DMA((2,)),
                            pltpu.VMEM((8, n_sub * tk + LANES), jnp.int32)]
            + [pltpu.VMEM((rows, LANES), F32)] * 3),
        out_shape=jax.ShapeDtypeStruct((n_seq, rows, LANES), F32),
        compiler_params=_cparams(("arbitrary",)),
        name="dsa_sample_attention",
    )(page_table, idx_t, kv_t, iqr, iw, q, itail, ktail)
    o = out.reshape(n_seq, t_new, DSA_HEADS, 2, HEAD_DIM)
    heads = jnp.arange(DSA_HEADS)
    o = o[:, :, heads, heads // DSA_GROUP, :]
    return o.reshape(n_seq * t_new, DSA_HEADS * HEAD_DIM).astype(BF16)


def _fox_sample_kernel(pt_ref, kv_cache, lf_cache, q_ref, ktail_ref, ltail_ref, o_ref,
                       kbuf, lbuf, ksem, lsem, m_ref, l_ref, acc_ref, carry_ref, colg_ref,
                       *, li, n_sub, ppc, n_chunk, t_new):
    s, c = pl.program_id(0), pl.program_id(1)
    step = s * n_chunk + c
    slot = step % 2
    cpp = n_sub * ppc

    def dma(seq, chunk, slot, wait):
        page0 = (n_chunk - 1 - chunk) * cpp
        _page_dma(kv_cache, li, pt_ref, seq, page0, n_sub, ppc, kbuf, slot, ksem, wait)
        _page_dma(lf_cache, li, pt_ref, seq, page0, n_sub, ppc, lbuf, slot, lsem, wait)

    @pl.when(step == 0)
    def _():
        dma(s, c, slot, False)

    @pl.when(step + 1 < pl.num_programs(0) * n_chunk)
    def _():
        nxt = step + 1
        dma(nxt // n_chunk, nxt % n_chunk, 1 - slot, False)

    rows = t_new * FOX_HEADS
    q = q_ref[0]
    sb = min(256, kbuf.shape[-1])
    later = (lax.broadcasted_iota(jnp.int32, (sb, sb), 0) > lax.broadcasted_iota(jnp.int32, (sb, sb), 1)).astype(BF16)

    def later_sums(lf_t):
        w = lf_t.shape[-1]
        bw = min(sb, w)
        parts = jnp.concatenate(_split3(lf_t), axis=0)
        carry = carry_ref[...]
        gs = [None] * (w // bw)
        for jb in reversed(range(w // bw)):
            sfx = _dot(parts[:, jb * bw:(jb + 1) * bw], later[0:bw, 0:bw])
            gs[jb] = sfx[0:FOX_HEADS] + sfx[FOX_HEADS:2 * FOX_HEADS] + sfx[2 * FOX_HEADS:3 * FOX_HEADS] + carry
            carry = carry + jnp.sum(lf_t[:, jb * bw:(jb + 1) * bw], axis=-1, keepdims=True)
        carry_ref[...] = carry
        return jnp.concatenate(gs, axis=1) * LOG2E

    def block(k_t, v_t, lf_t, mask, first):
        g = later_sums(lf_t)
        if first:
            colg_ref[...] = jnp.concatenate([g[:, qq:qq + 1] for qq in range(t_new)], axis=0)
        sc = _dot(q, k_t.astype(BF16)) + jnp.concatenate([g] * t_new, axis=0) - colg_ref[...]
        if mask is not None:
            sc = jnp.where(mask, sc, NEG)
        _flash_step(sc, v_t.astype(BF16), m_ref, l_ref, acc_ref, nt=True)

    @pl.when(c == 0)
    def _():
        _flash_init(m_ref, l_ref, acc_ref)
        carry_ref[...] = jnp.zeros(carry_ref.shape, F32)
        block(ktail_ref[0, 0:512], ktail_ref[0, 512:1024], ltail_ref[0], _tail_mask(rows, FOX_HEADS), True)

    dma(s, c, slot, True)
    block(kbuf[slot, 0, 0:512], kbuf[slot, 0, 512:1024], lbuf[slot, 0], None, False)

    @pl.when(c == n_chunk - 1)
    def _():
        o_ref[0] = _flash_out(l_ref, acc_ref)


def _fox_sample(page_table, kv_t, lf_t, li, q, kv_new, lf_new, n_seq):
    t_new = q.shape[0] // n_seq
    rows = t_new * FOX_HEADS
    n_pages = page_table.shape[1]
    page = kv_t.shape[-1]
    ppc = _sub_pages(n_pages, 16)
    n_sub, n_chunk = 1, n_pages // ppc
    tk = ppc * page
    heads = jnp.arange(FOX_HEADS)
    place = (heads[:, None] // 4 == jnp.arange(4)[None, :]).astype(q.dtype)
    qpad = q.reshape(n_seq, t_new, FOX_HEADS, 1, FQ_W)[..., :LANES]
    qbd = (qpad * place[None, None, :, :, None]).reshape(n_seq, rows, 4 * LANES)
    ktail = _feature_major_tail(kv_new, n_seq)
    ltail = _feature_major_tail(lf_new, n_seq)
    seq3 = lambda a: pl.BlockSpec((1,) + a.shape[1:], lambda s, c, pt: (s, 0, 0))
    out = pl.pallas_call(
        functools.partial(_fox_sample_kernel, li=li, n_sub=n_sub, ppc=ppc, n_chunk=n_chunk, t_new=t_new),
        grid_spec=pltpu.PrefetchScalarGridSpec(
            num_scalar_prefetch=1, grid=(n_seq, n_chunk),
            in_specs=[pl.BlockSpec(memory_space=pl.ANY), pl.BlockSpec(memory_space=pl.ANY),
                      seq3(qbd), seq3(ktail), seq3(ltail)],
            out_specs=pl.BlockSpec((1, rows, 512), lambda s, c, pt: (s, 0, 0)),
            scratch_shapes=[pltpu.VMEM((2, n_sub, 1024, tk), F32), pltpu.VMEM((2, n_sub, FOX_HEADS, tk), F32),
                            pltpu.SemaphoreType.DMA((2,)), pltpu.SemaphoreType.DMA((2,)),
                            pltpu.VMEM((rows, LANES), F32), pltpu.VMEM((rows, LANES), F32),
                            pltpu.VMEM((rows, 512), F32),
                            pltpu.VMEM((FOX_HEADS, 1), F32), pltpu.VMEM((rows, 1), F32)]),
        out_shape=jax.ShapeDtypeStruct((n_seq, rows, 512), F32),
        compiler_params=_cparams(("arbitrary", "arbitrary")),
        name="fox_sample_attention",
    )(page_table, kv_t, lf_t, qbd, ktail, ltail)
    o = out.reshape(n_seq, t_new, FOX_HEADS, FOX_KV_HEADS, HEAD_DIM)
    o = o[:, :, heads, heads // 2, :]
    return o.reshape(n_seq * t_new, FOX_HEADS * HEAD_DIM).astype(BF16)


def _prompt_even(xp, t_real, tab, g_mix, ew, topk):
    qm, km, vm, lat, qd, kvd, kdb, iq, misc = _even_project(xp, t_real, tab, g_mix, ew)
    om = _mla_prompt(qm, km, vm)
    od = _dsa_prompt(qd, iq, misc, kdb, topk)
    return om, od, lat, kvd, misc[:, :t_real, 0:IDX_DIM]


def _prompt_odd(xp, t_real, g_mix, ow):
    q, kv, kvb, lf = _odd_project(xp, t_real, g_mix, ow)
    return _fox_prompt(q, kvb), kv, lf


def kernel(x_prompt, x_sample, cache_mla, cache_dsa_kv, cache_dsa_idx, cache_fox_kv, cache_fox_logf, page_table, meta_tokens, g_mix, g_ffn, w_in_even, g_mla_q_lat, w_mla_uq, g_mla_kv_lat, w_mla_uk, w_mla_uv, g_mla_q, g_mla_k, g_dsa_q, g_dsa_k, g_idx_k, w_out_even, w_in_odd, b_fox_f, g_fox_q, g_fox_k, w_out_odd, w_ffn_gate, w_ffn_up, w_ffn_down):
    b, s, d = x_prompt.shape
    depth = g_mix.shape[0]
    t_real = s + N_META
    tp = -(-t_real // LANES) * LANES
    meta = jnp.broadcast_to(meta_tokens.astype(x_prompt.dtype)[None], (b, N_META, d))
    xp = jnp.concatenate([meta, x_prompt, jnp.zeros((b, tp - t_real, d), x_prompt.dtype)], axis=1)
    tab_p = _rope_tables(jnp.arange(tp))
    topk_p = min(TOPK_MAX, s // 4)

    n_seq, t_new, _ = x_sample.shape
    n_rows = n_seq * t_new
    past = page_table.shape[1] * cache_mla.shape[2]
    xs = x_sample.reshape(1, n_rows, d)
    tab_s = _rope_tables(past + jnp.arange(n_rows) % t_new)
    topk_s = min(TOPK_MAX, (past + t_new) // 4)
    mla_t = jnp.swapaxes(cache_mla, 2, 3)
    dkv_t = jnp.transpose(cache_dsa_kv, (0, 1, 3, 4, 5, 2)).reshape(cache_dsa_kv.shape[:2] + (256, -1))
    idx_t = jnp.swapaxes(cache_dsa_idx, 2, 3)
    fkv_t = jnp.transpose(cache_fox_kv, (0, 1, 3, 4, 5, 2)).reshape(cache_fox_kv.shape[:2] + (1024, -1))
    flf_t = jnp.swapaxes(cache_fox_logf, 2, 3)

    mla_p, dkv_p, idx_p, fkv_p, lf_p = [], [], [], [], []
    mla_s, dkv_s, idx_s, fkv_s, lf_s = [], [], [], [], []
    for l in range(depth):
        li = l // 2
        wg, wu, wd = w_ffn_gate[l].astype(BF16), w_ffn_up[l].astype(BF16), w_ffn_down[l].astype(BF16)
        if l % 2 == 0:
            ew = _even_weights(w_in_even[li], g_mla_q_lat[li], w_mla_uq[li], g_mla_kv_lat[li], w_mla_uk[li],
                               w_mla_uv[li], g_mla_q[li], g_mla_k[li], g_dsa_q[li], g_dsa_k[li], g_idx_k[li])
            w_out = w_out_even[li].astype(BF16)
            w_outs = (w_out[:512], w_out[512:])
            om, od, lat, kvd, ik = _prompt_even(xp, t_real, tab_p, g_mix[l], ew, topk_p)
            mla_p.append(lat)
            dkv_p.append(kvd.reshape(b, t_real, 2, DSA_KV_HEADS, HEAD_DIM))
            idx_p.append(ik)
            xp = _mix_ffn(xp, (om, od), w_outs, g_ffn[l], wg, wu, wd)

            qm, _, _, lat, qd, kvd, _, iq, misc = _even_project(xs, n_rows, tab_s, g_mix[l], ew)
            wukt = jnp.transpose(w_mla_uk[li], (1, 2, 0)).reshape(MLA_HEADS * MLA_NOPE, MLA_KV_LORA).astype(BF16)
            om = _mla_sample(page_table, mla_t, li, qm[0], lat[0], wukt, ew[5], n_seq)
            od = _dsa_sample(page_table, idx_t, dkv_t, li, qd[0], iq[0], misc[0], kvd[0], topk_s, n_seq)
            mla_s.append(lat.reshape(n_seq, t_new, MLA_LAT))
            dkv_s.append(kvd.reshape(n_seq, t_new, 2, DSA_KV_HEADS, HEAD_DIM))
            idx_s.append(misc[0, :, 0:IDX_DIM].reshape(n_seq, t_new, IDX_DIM))
            xs = _mix_ffn(xs, (om[None], od[None]), w_outs, g_ffn[l], wg, wu, wd)
        else:
            ow = _odd_weights(w_in_odd[li], b_fox_f[li], g_fox_q[li], g_fox_k[li])
            w_outs = (w_out_odd[li].astype(BF16),)
            of, kv, lf = _prompt_odd(xp, t_real, g_mix[l], ow)
            fkv_p.append(kv.reshape(b, t_real, 2, FOX_KV_HEADS, HEAD_DIM))
            lf_p.append(lf)
            xp = _mix_ffn(xp, (of,), w_outs, g_ffn[l], wg, wu, wd)

            q, kv, _, lf = _odd_project(xs, n_rows, g_mix[l], ow)
            of = _fox_sample(page_table, fkv_t, flf_t, li, q[0], kv[0], lf[0], n_seq)
            fkv_s.append(kv.reshape(n_seq, t_new, 2, FOX_KV_HEADS, HEAD_DIM))
            lf_s.append(lf.reshape(n_seq, t_new, FOX_HEADS))
            xs = _mix_ffn(xs, (of[None],), w_outs, g_ffn[l], wg, wu, wd)
    y_prompt = xp[:, N_META:t_real]
    y_sample = xs.reshape(n_seq, t_new, d)
    return (y_prompt, y_sample, jnp.stack(mla_p), jnp.stack(mla_s), jnp.stack(dkv_p), jnp.stack(dkv_s),
            jnp.stack(idx_p), jnp.stack(idx_s), jnp.stack(fkv_p), jnp.stack(fkv_s), jnp.stack(lf_p), jnp.stack(lf_s))
```

```python
import functools

import numpy as np
import jax
import jax.numpy as jnp
from jax import lax
from jax.experimental import pallas as pl
from jax.experimental.pallas import tpu as pltpu

F32 = jnp.float32
BF16 = jnp.bfloat16

N_META = 16
HEAD_DIM = 64
ROPE_THETA = 10000.0
NORM_EPS = 1e-6
MLA_HEADS = 8
MLA_NOPE = 64
MLA_ROPE = 32
MLA_QK = MLA_NOPE + MLA_ROPE
MLA_V = 64
MLA_Q_LORA = 384
MLA_KV_LORA = 256
MLA_LAT = MLA_KV_LORA + MLA_ROPE
DSA_HEADS = 8
DSA_KV_HEADS = 2
DSA_GROUP = DSA_HEADS // DSA_KV_HEADS
IDX_HEADS = 8
IDX_DIM = 64
TOPK_MAX = 256
IDX_SCALE = (IDX_DIM * IDX_HEADS) ** -0.5
FOX_HEADS = 16
FOX_KV_HEADS = 8

LANES = 128
VMEM_LIMIT = 56 * 1024 * 1024
NEG = -1e30
LOG2E = 1.4426950408889634
INT_MIN = -(2 ** 31)

P_QL, P_CKV, P_DQ, P_DK, P_DV, P_IQ, P_MISC, P_END = 0, 384, 640, 1152, 1280, 1408, 1920, 2048
M_KPE, M_IW = 64, 96


def _pick_tile(n):
    for t in (384, 256, 128):
        if n % t == 0:
            return t
    raise ValueError(f"row count {n} is not a multiple of 128")


def _cparams(sem):
    return pltpu.CompilerParams(dimension_semantics=sem, vmem_limit_bytes=VMEM_LIMIT)


def _lane(shape):
    return lax.broadcasted_iota(jnp.int32, shape, len(shape) - 1)


def _partner(x, d):
    lane = _lane(x.shape)
    return jnp.where((lane & d) != 0, pltpu.roll(x, d, 1), pltpu.roll(x, LANES - d, 1))


def _seg_ones(seg):
    r = lax.broadcasted_iota(jnp.int32, (LANES, LANES), 0) // seg
    c = lax.broadcasted_iota(jnp.int32, (LANES, LANES), 1) // seg
    return (r == c).astype(BF16)


def _seg_sumsq(x, ones_blk):
    sq = x * x
    hi = sq.astype(BF16)
    lo = (sq - hi.astype(F32)).astype(BF16)
    return _dot(hi, ones_blk) + _dot(lo, ones_blk)


def _swap_halves(x):
    return pltpu.roll(x, 64, 1)


def _rms_rows(x, g):
    ss = jnp.sum(x * x, axis=-1, keepdims=True)
    return x * lax.rsqrt(ss / x.shape[-1] + NORM_EPS) * g


def _dot(a, b):
    return jnp.dot(a, b, preferred_element_type=F32)


def _dot_nt(a, b):
    return lax.dot_general(a, b, (((1,), (1,)), ((), ())), preferred_element_type=F32)


def _even_proj_kernel(x_ref, tab_ref, gmix_ref, win_ref, gql_ref, wuq_ref, gkv_ref, wuk_ref, wuv_ref,
                      gq_ref, gdq_ref, gdk_ref, gmisc_ref,
                      qm_ref, km_ref, vm_ref, lat_ref, qd_ref, kvd_ref, kdb_ref, iq_ref, misc_ref):
    x = x_ref[0]
    tm = x.shape[0]
    h = _rms_rows(x, gmix_ref[...]).astype(BF16)
    p = _dot(h, win_ref[...])
    lane = _lane((tm, LANES))
    lo = lane < 64
    cos64, sin64 = tab_ref[:, 0:128], tab_ref[:, 128:256]
    cosq, sinq = tab_ref[:, 256:384], tab_ref[:, 384:512]
    cosm, sinm = tab_ref[:, 512:640], tab_ref[:, 640:768]

    ql = _rms_rows(p[:, P_QL:P_CKV], gql_ref[...]).astype(BF16)
    qm = _dot(ql, wuq_ref[...])
    gq = gq_ref[...]
    for hh in range(MLA_HEADS):
        c = qm[:, LANES * hh:LANES * (hh + 1)]
        c = c * cosq + _partner(c, 16) * sinq
        ss = jnp.sum(c * c, axis=-1, keepdims=True)
        c = c * lax.rsqrt(ss / MLA_QK + NORM_EPS) * gq
        qm_ref[0, :, LANES * hh:LANES * (hh + 1)] = c.astype(BF16)

    ckv = _rms_rows(p[:, P_CKV:P_DQ], gkv_ref[...])
    lat_ref[0, :, 0:MLA_KV_LORA] = ckv

    m = p[:, P_MISC:P_END]
    ikss = jnp.sum(jnp.where(lo, m * m, 0.0), axis=-1, keepdims=True)
    m = m * jnp.where(lo, lax.rsqrt(ikss / IDX_DIM + NORM_EPS) * gmisc_ref[...], 1.0)
    rot = jnp.where(lo, _partner(m, 32), _partner(m, 16))
    m = m * cosm + rot * sinm
    misc_ref[0] = m
    lat_ref[0, :, MLA_KV_LORA:MLA_LAT] = m[:, M_KPE:M_KPE + MLA_ROPE]

    ckv_b = ckv.astype(BF16)
    kn = _dot(ckv_b, wuk_ref[...])
    kpe = jnp.where((lane >= M_KPE) & (lane < M_KPE + MLA_ROPE), m, 0.0)
    for hh in range(MLA_HEADS):
        c = kn[:, LANES * hh:LANES * (hh + 1)] + kpe
        ss = jnp.sum(c * c, axis=-1, keepdims=True)
        km_ref[0, :, LANES * hh:LANES * (hh + 1)] = (c * lax.rsqrt(ss / MLA_QK + NORM_EPS)).astype(BF16)
    vm_ref[0] = _dot(ckv_b, wuv_ref[...]).astype(BF16)

    gdq = gdq_ref[...]
    ones64 = _seg_ones(HEAD_DIM)
    for j in range(DSA_HEADS // 2):
        c = p[:, P_DQ + LANES * j:P_DQ + LANES * (j + 1)]
        c = c * lax.rsqrt(_seg_sumsq(c, ones64) / HEAD_DIM + NORM_EPS) * gdq
        c = c * cos64 + _partner(c, 32) * sin64
        sw = _swap_halves(c)
        if (2 * j) // DSA_GROUP == 0:
            a, b = jnp.where(lo, c, 0.0), jnp.where(lo, sw, 0.0)
        else:
            a, b = jnp.where(lo, 0.0, sw), jnp.where(lo, 0.0, c)
        qd_ref[0, :, LANES * (2 * j):LANES * (2 * j + 1)] = a.astype(BF16)
        qd_ref[0, :, LANES * (2 * j + 1):LANES * (2 * j + 2)] = b.astype(BF16)

    c = p[:, P_DK:P_DV]
    c = c * lax.rsqrt(_seg_sumsq(c, ones64) / HEAD_DIM + NORM_EPS) * gdk_ref[...]
    kd = c * cos64 + _partner(c, 32) * sin64
    vd = p[:, P_DV:P_IQ]
    kvd_ref[0, :, 0:128] = kd
    kvd_ref[0, :, 128:256] = vd
    kdb_ref[0, :, 0:128] = kd.astype(BF16)
    kdb_ref[0, :, 128:256] = vd.astype(BF16)
    kdb_ref[0, :, 256:384] = m.astype(BF16)

    for j in range(IDX_HEADS // 2):
        c = p[:, P_IQ + LANES * j:P_IQ + LANES * (j + 1)]
        c = c * cos64 + _partner(c, 32) * sin64
        iq_ref[0, :, LANES * (2 * j):LANES * (2 * j + 1)] = jnp.where(lo, c, 0.0).astype(BF16)
        iq_ref[0, :, LANES * (2 * j + 1):LANES * (2 * j + 2)] = jnp.where(lo, _swap_halves(c), 0.0).astype(BF16)


def _rope_tables(pos):
    pos = pos.astype(F32)[:, None]

    def cs(half):
        inv = ROPE_THETA ** (-jnp.arange(half, dtype=F32) / half)
        ang = pos * inv
        c, s = jnp.cos(ang), jnp.sin(ang)
        return jnp.concatenate([c, c], -1), jnp.concatenate([-s, s], -1)

    c64, s64 = cs(32)
    c32, s32 = cs(16)
    t = pos.shape[0]
    one, zero = jnp.ones((t, 32), F32), jnp.zeros((t, 32), F32)
    cos64, sin64 = jnp.concatenate([c64, c64], -1), jnp.concatenate([s64, s64], -1)
    cosq = jnp.concatenate([one, one, c32, one], -1)
    sinq = jnp.concatenate([zero, zero, s32, zero], -1)
    cosm = jnp.concatenate([c64, c32, one], -1)
    sinm = jnp.concatenate([s64, s32, zero], -1)
    return jnp.concatenate([cos64, sin64, cosq, sinq, cosm, sinm], -1)


def _even_weights(w_in, g_q_lat, w_uq, g_kv_lat, w_uk, w_uv, g_mq, g_mk, g_dq, g_dk, g_ik):
    d = w_in.shape[0]
    o_dsa = MLA_Q_LORA + MLA_KV_LORA + MLA_ROPE
    o_dq, o_dk, o_dv, o_iq = o_dsa, o_dsa + 512, o_dsa + 640, o_dsa + 768
    o_iw, o_ik = o_iq + 512, o_iq + 520
    win = jnp.concatenate([
        w_in[:, 0:640], w_in[:, o_dq:o_iw], w_in[:, o_ik:o_ik + 64], w_in[:, 640:672], w_in[:, o_iw:o_iw + 8],
        jnp.zeros((d, 24), w_in.dtype)], axis=1).astype(BF16)
    wuq = w_uq.reshape(MLA_Q_LORA, MLA_HEADS, MLA_QK)
    wuq = jnp.pad(wuq, ((0, 0), (0, 0), (0, LANES - MLA_QK))).reshape(MLA_Q_LORA, MLA_HEADS * LANES).astype(BF16)
    wuk = jnp.pad(w_uk, ((0, 0), (0, 0), (0, LANES - MLA_NOPE))).reshape(MLA_KV_LORA, MLA_HEADS * LANES).astype(BF16)
    wuv = w_uv.reshape(MLA_KV_LORA, MLA_HEADS * MLA_V).astype(BF16)
    gq = jnp.concatenate([g_mq * g_mk * (MLA_QK ** -0.5 * LOG2E), jnp.zeros((LANES - MLA_QK,), F32)])[None]
    gdq = (jnp.concatenate([g_dq, g_dq]) * (HEAD_DIM ** -0.5 * LOG2E))[None]
    gdk = jnp.concatenate([g_dk, g_dk])[None]
    gmisc = jnp.concatenate([g_ik, jnp.ones((64,), F32)])[None]
    return win, g_q_lat[None], wuq, g_kv_lat[None], wuk, wuv, gq, gdq, gdk, gmisc


def _even_project(x, t_out, tab, g_mix, ew):
    b, tp, d = x.shape
    tm = _pick_tile(tp)
    win, gql, wuq, gkv, wuk, wuv, gq, gdq, gdk, gmisc = ew
    row = lambda w: pl.BlockSpec((1, tm, w), lambda i, j: (i, j, 0))
    full = lambda a: pl.BlockSpec(a.shape, lambda i, j: (0,) * a.ndim)
    outs = [
        jax.ShapeDtypeStruct((b, tp, 1024), BF16), jax.ShapeDtypeStruct((b, tp, 1024), BF16),
        jax.ShapeDtypeStruct((b, tp, 512), BF16), jax.ShapeDtypeStruct((b, t_out, MLA_LAT), F32),
        jax.ShapeDtypeStruct((b, tp, 1024), BF16), jax.ShapeDtypeStruct((b, t_out, 256), F32),
        jax.ShapeDtypeStruct((b, tp, 384), BF16), jax.ShapeDtypeStruct((b, tp, 1024), BF16),
        jax.ShapeDtypeStruct((b, tp, LANES), F32),
    ]
    params = (g_mix[None], win, gql, wuq, gkv, wuk, wuv, gq, gdq, gdk, gmisc)
    return pl.pallas_call(
        _even_proj_kernel,
        grid=(b, tp // tm),
        in_specs=[row(d), pl.BlockSpec((tm, 768), lambda i, j: (j, 0))] + [full(a) for a in params],
        out_specs=[row(1024), row(1024), row(512), row(MLA_LAT), row(1024), row(256), row(384), row(1024), row(LANES)],
        out_shape=outs,
        compiler_params=_cparams(("parallel", "parallel")),
        name="even_project",
    )(x, tab, *params)


def _flash_init(m_ref, l_ref, acc_ref):
    m_ref[...] = jnp.full(m_ref.shape, NEG, F32)
    l_ref[...] = jnp.zeros(l_ref.shape, F32)
    acc_ref[...] = jnp.zeros(acc_ref.shape, F32)


def _flash_step(s, v, m_ref, l_ref, acc_ref, nt=False):
    tk = s.shape[1]
    tiles = [s[:, LANES * j:LANES * (j + 1)] for j in range(tk // LANES)]
    mx = functools.reduce(jnp.maximum, tiles)
    m_prev = m_ref[...]
    m_new = jnp.maximum(m_prev, jnp.max(mx, axis=-1, keepdims=True))
    alpha = jnp.exp2(m_prev - m_new)
    ps = [jnp.exp2(tl - m_new) for tl in tiles]
    l_ref[...] = alpha * l_ref[...] + functools.reduce(jnp.add, ps)
    p = jnp.concatenate(ps, axis=1).astype(BF16)
    pv = _dot_nt(p, v) if nt else _dot(p, v)
    n = pv.shape[1]
    a = alpha if n == LANES else jnp.concatenate([alpha] * (n // LANES), axis=1)
    acc_ref[...] = a * acc_ref[...] + pv
    m_ref[...] = m_new


def _flash_out(l_ref, acc_ref):
    return acc_ref[...] / jnp.sum(l_ref[...], axis=-1, keepdims=True)


def _causal_tile_mask(rows, t):
    r = lax.broadcasted_iota(jnp.int32, (rows, t), 0) % t
    return r >= lax.broadcasted_iota(jnp.int32, (rows, t), 1)


def _mla_prompt_kernel(q_ref, k_ref, v_ref, o_ref, m_ref, l_ref, acc_ref, *, t):
    qi = pl.program_id(1)
    diag = _causal_tile_mask(2 * t, t)
    lo = _lane((t, LANES)) < 64
    _flash_init(m_ref, l_ref, acc_ref)

    def step(j, masked):
        rows = pl.ds(pl.multiple_of(j * t, t), t)
        for pair in range(MLA_HEADS // 2):
            s = jnp.concatenate([_dot_nt(q_ref[0, :, LANES * hh:LANES * (hh + 1)],
                                         k_ref[0, rows, LANES * hh:LANES * (hh + 1)])
                                 for hh in (2 * pair, 2 * pair + 1)], axis=0)
            if masked:
                s = jnp.where(diag, s, NEG)
            _flash_step(s, v_ref[0, rows, LANES * pair:LANES * (pair + 1)],
                        m_ref.at[pair], l_ref.at[pair], acc_ref.at[pair])

    lax.fori_loop(0, qi, lambda j, c: (step(j, False), c)[1], 0)
    step(qi, True)
    for pair in range(MLA_HEADS // 2):
        out = _flash_out(l_ref.at[pair], acc_ref.at[pair])
        o_ref[0, :, LANES * pair:LANES * (pair + 1)] = jnp.where(lo, out[0:t], out[t:2 * t]).astype(o_ref.dtype)


def _mla_prompt(qm, km, vm):
    b, tp, _ = qm.shape
    t = _pick_tile(tp)
    return pl.pallas_call(
        functools.partial(_mla_prompt_kernel, t=t),
        grid=(b, tp // t),
        in_specs=[pl.BlockSpec((1, t, 1024), lambda i, j: (i, j, 0)),
                  pl.BlockSpec((1, tp, 1024), lambda i, j: (i, 0, 0)),
                  pl.BlockSpec((1, tp, 512), lambda i, j: (i, 0, 0))],
        out_specs=pl.BlockSpec((1, t, 512), lambda i, j: (i, j, 0)),
        out_shape=jax.ShapeDtypeStruct((b, tp, 512), BF16),
        scratch_shapes=[pltpu.VMEM((MLA_HEADS // 2, 2 * t, LANES), F32)] * 3,
        compiler_params=_cparams(("parallel", "arbitrary")),
        name="mla_prompt_attention",
    )(qm, km, vm)


def _sort_key(score):
    bits = pltpu.bitcast(score + 0.0, jnp.int32)
    return jnp.where(bits < 0, bits ^ 0x7FFFFFFF, bits)


def _row_count(pred):
    return jnp.sum(pred.astype(F32), axis=-1, keepdims=True)


def _topk_threshold(count, rows, k, idx_bits):
    kf = jnp.float32(k)
    few = count(lambda key, idx, sl: key > INT_MIN) <= kf
    c0 = count(lambda key, idx, sl: key >= 0)
    ok0 = c0 >= kf
    t0 = jnp.where(ok0, 0, INT_MIN).astype(jnp.int32)
    n0 = jnp.where(ok0, c0, jnp.float32(2.0 ** 30))

    def settled(n_ge):
        return jnp.min(jnp.where((n_ge == kf) | few, 1.0, 0.0)) > 0.0

    def key_bit(state):
        i, t, n_ge = state
        cand = t | (jnp.int32(1) << (30 - i))
        c = count(lambda key, idx, sl: key >= cand[sl])
        ok = c >= kf
        return i + 1, jnp.where(ok, cand, t), jnp.where(ok, c, n_ge)

    _, thr, n_ge = lax.while_loop(lambda st: (st[0] < 31) & jnp.logical_not(settled(st[2])), key_bit,
                                  (jnp.int32(0), t0, n0))
    all_idx = jnp.full((rows, 1), (1 << idx_bits) - 1, jnp.int32)

    def tie_bound():
        need = kf - count(lambda key, idx, sl: key > thr[sl])

        def idx_bit(i, hi):
            cand = hi & ~(jnp.int32(1) << (idx_bits - 1 - i))
            ok = count(lambda key, idx, sl: (key == thr[sl]) & (idx <= cand[sl])) >= need
            return jnp.where(ok, cand, hi)

        return lax.fori_loop(0, idx_bits, idx_bit, all_idx)

    has_tie = jnp.max(jnp.where((n_ge > kf) & (thr > INT_MIN) & jnp.logical_not(few), 1.0, 0.0)) > 0.0
    return thr, lax.cond(has_tie, tie_bound, lambda: all_idx)


def _selected(key, idx, thr, bound):
    return ((key > thr) | ((key == thr) & (idx <= bound))) & (key > INT_MIN)


def _dsa_prompt_kernel(q_ref, iq_ref, misc_ref, kdb_ref, o_ref, key_ref, bias_ref, m_ref, l_ref, acc_ref,
                       *, t, topk, idx_bits):
    qi = pl.program_id(1)
    nk = qi + 1
    row_pos = qi * t + lax.broadcasted_iota(jnp.int32, (t, t), 0)
    col_iota = lax.broadcasted_iota(jnp.int32, (t, t), 1)
    iw = misc_ref[0][:, M_IW:M_IW + IDX_HEADS] * IDX_SCALE

    def score_chunk(c, carry):
        rows = pl.ds(pl.multiple_of(c * t, t), t)
        ik = kdb_ref[0, rows, 256:384]
        acc = jnp.zeros((t, t), F32)
        for hh in range(IDX_HEADS):
            d = _dot_nt(iq_ref[0, :, LANES * hh:LANES * (hh + 1)], ik)
            acc = acc + iw[:, hh:hh + 1] * jnp.maximum(d, 0.0)
        ok = row_pos >= c * t + col_iota
        key_ref[c] = jnp.where(ok, _sort_key(acc), INT_MIN)
        return carry

    lax.fori_loop(0, nk, score_chunk, 0)

    def count(f):
        def body(c, acc):
            hit = f(key_ref[c], c * t + col_iota, slice(None)).astype(F32)
            return acc + functools.reduce(jnp.add, [hit[:, LANES * j:LANES * (j + 1)] for j in range(t // LANES)])
        return jnp.sum(lax.fori_loop(0, nk, body, jnp.zeros((t, LANES), F32)), axis=-1, keepdims=True)

    thr, bound = _topk_threshold(count, t, topk, idx_bits)

    _flash_init(m_ref, l_ref, acc_ref)

    def attend_chunk(c, carry):
        rows = pl.ds(pl.multiple_of(c * t, t), t)
        bias_ref[...] = jnp.where(_selected(key_ref[c], c * t + col_iota, thr, bound), 0.0, NEG)
        kc = kdb_ref[0, rows, 0:128]
        vc = kdb_ref[0, rows, 128:256]
        for g in range(DSA_KV_HEADS):
            s = jnp.concatenate([_dot_nt(q_ref[0, :, LANES * hh:LANES * (hh + 1)], kc) + bias_ref[...]
                                 for hh in range(DSA_GROUP * g, DSA_GROUP * (g + 1))], axis=0)
            _flash_step(s, vc, m_ref.at[g], l_ref.at[g], acc_ref.at[g])
        return carry

    lax.fori_loop(0, nk, attend_chunk, 0)

    lo = _lane((t, LANES)) < 64
    for g in range(DSA_KV_HEADS):
        out = _flash_out(l_ref.at[g], acc_ref.at[g])
        for i in range(DSA_GROUP // 2):
            a, b = out[2 * i * t:(2 * i + 1) * t], out[(2 * i + 1) * t:(2 * i + 2) * t]
            pair = jnp.where(lo, a, _swap_halves(b)) if g == 0 else jnp.where(lo, _swap_halves(a), b)
            col = LANES * (g * DSA_GROUP // 2 + i)
            o_ref[0, :, col:col + LANES] = pair.astype(o_ref.dtype)


def _dsa_prompt(qd, iq, misc, kdb, topk):
    b, tp, _ = qd.shape
    t = _pick_tile(tp)
    n = tp // t
    idx_bits = max(1, int(tp - 1).bit_length())
    return pl.pallas_call(
        functools.partial(_dsa_prompt_kernel, t=t, topk=topk, idx_bits=idx_bits),
        grid=(b, n),
        in_specs=[pl.BlockSpec((1, t, 1024), lambda i, j: (i, j, 0)),
                  pl.BlockSpec((1, t, 1024), lambda i, j: (i, j, 0)),
                  pl.BlockSpec((1, t, LANES), lambda i, j: (i, j, 0)),
                  pl.BlockSpec((1, tp, 384), lambda i, j: (i, 0, 0))],
        out_specs=pl.BlockSpec((1, t, 512), lambda i, j: (i, j, 0)),
        out_shape=jax.ShapeDtypeStruct((b, tp, 512), BF16),
        scratch_shapes=[pltpu.VMEM((n, t, t), jnp.int32), pltpu.VMEM((t, t), F32)]
        + [pltpu.VMEM((DSA_KV_HEADS, DSA_GROUP * t, LANES), F32)] * 3,
        compiler_params=_cparams(("parallel", "arbitrary")),
        name="dsa_prompt_attention",
    )(qd, iq, misc, kdb)


def _mix_ffn_kernel(*refs, n_mix):
    x_ref = refs[0]
    o_refs = refs[1:1 + n_mix]
    w_refs = refs[1 + n_mix:1 + 2 * n_mix]
    g_ref, wg_ref, wu_ref, wd_ref, y_ref = refs[1 + 2 * n_mix:]
    x = x_ref[0]
    for o_ref, w_ref in zip(o_refs, w_refs):
        x = x + _dot(o_ref[0], w_ref[...])
    h = _rms_rows(x, g_ref[...]).astype(BF16)
    gate = _dot(h, wg_ref[...])
    up = _dot(h, wu_ref[...])
    act = (gate * jax.nn.sigmoid(gate) * up).astype(BF16)
    y_ref[0] = x + _dot(act, wd_ref[...])


def _mix_ffn(x, mixes, w_outs, g, wg, wu, wd):
    b, tp, d = x.shape
    tm = _pick_tile(tp)
    n_mix = len(mixes)
    row = lambda w: pl.BlockSpec((1, tm, w), lambda i, j: (i, j, 0))
    full = lambda a: pl.BlockSpec(a.shape, lambda i, j: (0,) * a.ndim)
    params = (g[None], wg, wu, wd)
    return pl.pallas_call(
        functools.partial(_mix_ffn_kernel, n_mix=n_mix),
        grid=(b, tp // tm),
        in_specs=[row(d)] + [row(o.shape[-1]) for o in mixes] + [full(w) for w in w_outs] + [full(a) for a in params],
        out_specs=row(d),
        out_shape=jax.ShapeDtypeStruct((b, tp, d), F32),
        compiler_params=_cparams(("parallel", "parallel")),
        name="mix_ffn",
    )(x, *mixes, *w_outs, *params)


O_Q, O_K, O_V, O_F, O_END = 0, 1024, 1536, 2048, 2176


def _split3(x):
    a = x.astype(BF16)
    r = x - a.astype(F32)
    b = r.astype(BF16)
    c = (r - b.astype(F32)).astype(BF16)
    return a, b, c


FQ_W, FK_W, FV_OFF, FKV_W = 256, 256, 1024, 1536


def _odd_proj_kernel(x_ref, gmix_ref, win_ref, bf_ref, gq_ref, gk_ref,
                     q_ref, kv_ref, kvb_ref, lf_ref, carry_ref):
    @pl.when(pl.program_id(1) == 0)
    def _():
        carry_ref[...] = jnp.zeros(carry_ref.shape, F32)

    x = x_ref[0]
    tm = x.shape[0]
    h = _rms_rows(x, gmix_ref[...]).astype(BF16)
    p = _dot(h, win_ref[...])
    lane = _lane((tm, LANES))
    lo = lane < 64

    lf = jax.nn.log_sigmoid(p[:, O_F:O_END] + bf_ref[...])
    lf_ref[0] = lf[:, 0:FOX_HEADS]
    tri = (lax.broadcasted_iota(jnp.int32, (tm, tm), 0) >= lax.broadcasted_iota(jnp.int32, (tm, tm), 1)).astype(BF16)
    cum = carry_ref[0:1, :]
    for part in _split3(lf):
        cum = cum + _dot(tri, part)
    carry_ref[0:1, :] = cum[tm - 1:tm, :]
    heads = lane < FOX_HEADS
    c2 = cum * LOG2E
    hi = jnp.where(heads, c2.astype(BF16).astype(F32), 0.0)
    r1 = c2 - hi
    mid = jnp.where(heads, r1.astype(BF16).astype(F32), 0.0)
    low = jnp.where(heads, (r1 - mid).astype(BF16).astype(F32), 0.0)
    packed = hi + pltpu.roll(mid, 16, 1) + pltpu.roll(low, 32, 1)
    bias_key = (packed + jnp.where((lane >= 64) & (lane < 112), 1.0, 0.0)).astype(BF16)
    cq = _swap_halves(packed)

    gq, gk = gq_ref[...], gk_ref[...]
    ones64 = _seg_ones(HEAD_DIM)
    for j in range(FOX_HEADS // 2):
        c = p[:, O_Q + LANES * j:O_Q + LANES * (j + 1)]
        c = c * lax.rsqrt(_seg_sumsq(c, ones64) / HEAD_DIM + NORM_EPS) * gq
        sw = _swap_halves(c)
        if j % 2 == 0:
            pair = (jnp.where(lo, c, 0.0), jnp.where(lo, sw, 0.0))
        else:
            pair = (jnp.where(lo, 0.0, sw), jnp.where(lo, 0.0, c))
        for i, qh in enumerate(pair):
            hh = 2 * j + i
            minus = (lane == hh) | (lane == 16 + hh) | (lane == 32 + hh)
            own = (lane == 64 + hh) | (lane == 80 + hh) | (lane == 96 + hh)
            q_ref[0, :, FQ_W * hh:FQ_W * hh + LANES] = qh.astype(BF16)
            q_ref[0, :, FQ_W * hh + LANES:FQ_W * (hh + 1)] = jnp.where(minus, -1.0, jnp.where(own, cq, 0.0)).astype(BF16)
    for j in range(FOX_KV_HEADS // 2):
        c = p[:, O_K + LANES * j:O_K + LANES * (j + 1)]
        c = c * lax.rsqrt(_seg_sumsq(c, ones64) / HEAD_DIM + NORM_EPS) * gk
        kv_ref[0, :, LANES * j:LANES * (j + 1)] = c
        kvb_ref[0, :, FK_W * j:FK_W * j + LANES] = c.astype(BF16)
        kvb_ref[0, :, FK_W * j + LANES:FK_W * (j + 1)] = bias_key
    v = p[:, O_V:O_F]
    kv_ref[0, :, 512:1024] = v
    kvb_ref[0, :, FV_OFF:FKV_W] = v.astype(BF16)


def _odd_weights(w_in, b_f, g_q, g_k):
    d = w_in.shape[0]
    win = jnp.concatenate([w_in, jnp.zeros((d, O_END - w_in.shape[1]), w_in.dtype)], axis=1).astype(BF16)
    bf = jnp.concatenate([b_f, jnp.zeros((LANES - FOX_HEADS,), F32)])[None]
    gq = (jnp.concatenate([g_q, g_q]) * (HEAD_DIM ** -0.5 * LOG2E))[None]
    gk = jnp.concatenate([g_k, g_k])[None]
    return win, bf, gq, gk


def _odd_project(x, t_out, g_mix, ow):
    b, tp, d = x.shape
    tm = _pick_tile(tp)
    n = tp // tm
    win, bf, gq, gk = ow
    row = lambda w: pl.BlockSpec((1, tm, w), lambda i, j: (i, j, 0))
    full = lambda a: pl.BlockSpec(a.shape, lambda i, j: (0,) * a.ndim)
    params = (g_mix[None], win, bf, gq, gk)
    outs = [
        jax.ShapeDtypeStruct((b, tp, FOX_HEADS * FQ_W), BF16), jax.ShapeDtypeStruct((b, t_out, 1024), F32),
        jax.ShapeDtypeStruct((b, tp, FKV_W), BF16), jax.ShapeDtypeStruct((b, t_out, FOX_HEADS), F32),
    ]
    return pl.pallas_call(
        _odd_proj_kernel,
        grid=(b, n),
        in_specs=[row(d)] + [full(a) for a in params],
        out_specs=[row(FOX_HEADS * FQ_W), row(1024), row(FKV_W), row(FOX_HEADS)],
        out_shape=outs,
        scratch_shapes=[pltpu.VMEM((8, LANES), F32)],
        compiler_params=_cparams(("parallel", "arbitrary")),
        name="odd_project",
    )(x, *params)


def _fox_prompt_kernel(q_ref, kv_ref, o_ref, m_ref, l_ref, acc_ref, *, t):
    qi = pl.program_id(1)
    diag = _causal_tile_mask(2 * t, t)
    lo = _lane((t, LANES)) < 64
    _flash_init(m_ref, l_ref, acc_ref)

    def step(j, masked):
        rows = pl.ds(pl.multiple_of(j * t, t), t)
        for kvh in range(FOX_KV_HEADS):
            grp = kvh // 2
            k = kv_ref[0, rows, FK_W * grp:FK_W * (grp + 1)]
            s = jnp.concatenate([_dot_nt(q_ref[0, :, FQ_W * hh:FQ_W * (hh + 1)], k) for hh in (2 * kvh, 2 * kvh + 1)],
                                axis=0)
            if masked:
                s = jnp.where(diag, s, NEG)
            _flash_step(s, kv_ref[0, rows, FV_OFF + LANES * grp:FV_OFF + LANES * (grp + 1)],
                        m_ref.at[kvh], l_ref.at[kvh], acc_ref.at[kvh])

    lax.fori_loop(0, qi, lambda j, c: (step(j, False), c)[1], 0)
    step(qi, True)
    for kvh in range(FOX_KV_HEADS):
        out = _flash_out(l_ref.at[kvh], acc_ref.at[kvh])
        a, b = out[0:t], out[t:2 * t]
        pair = jnp.where(lo, a, _swap_halves(b)) if kvh % 2 == 0 else jnp.where(lo, _swap_halves(a), b)
        o_ref[0, :, LANES * kvh:LANES * (kvh + 1)] = pair.astype(o_ref.dtype)


def _fox_prompt(q, kvb):
    b, tp, _ = q.shape
    t = _pick_tile(tp)
    return pl.pallas_call(
        functools.partial(_fox_prompt_kernel, t=t),
        grid=(b, tp // t),
        in_specs=[pl.BlockSpec((1, t, FOX_HEADS * FQ_W), lambda i, j: (i, j, 0)),
                  pl.BlockSpec((1, tp, FKV_W), lambda i, j: (i, 0, 0))],
        out_specs=pl.BlockSpec((1, t, 1024), lambda i, j: (i, j, 0)),
        out_shape=jax.ShapeDtypeStruct((b, tp, 1024), BF16),
        scratch_shapes=[pltpu.VMEM((FOX_KV_HEADS, 2 * t, LANES), F32)] * 3,
        compiler_params=_cparams(("parallel", "arbitrary")),
        name="fox_prompt_attention",
    )(q, kvb)


def _page_dma(cache_ref, li, pt_ref, seq, page0, n_sub, ppc, buf_ref, slot, sem, wait):
    width = cache_ref.shape[-1]

    def body(sub, carry):
        for pp in range(ppc):
            page = pt_ref[seq, page0 + sub * ppc + pp]
            cp = pltpu.make_async_copy(cache_ref.at[li, page],
                                       buf_ref.at[slot, sub, :, pl.ds(pp * width, width)], sem.at[slot])
            if wait:
                cp.wait()
            else:
                cp.start()
        return carry

    lax.fori_loop(0, n_sub, body, 0)


def _tail_mask(rows, heads):
    r = lax.broadcasted_iota(jnp.int32, (rows, LANES), 0)
    return _lane((rows, LANES)) <= r // heads


def _feature_major_tail(x, n_seq):
    t_new = x.shape[0] // n_seq
    xt = jnp.swapaxes(x.reshape(n_seq, t_new, x.shape[1]), 1, 2)
    return jnp.pad(xt, ((0, 0), (0, 0), (0, LANES - t_new)))


def _mla_sample_kernel(pt_ref, cache_ref, qn_ref, qr_ref, tail_ref, wukt_ref, wuv_ref, o_ref,
                       buf, sem, m_ref, l_ref, acc_ref, *, li, n_sub, ppc, t_new):
    s = pl.program_id(0)
    slot = s % 2
    dma = functools.partial(_page_dma, cache_ref, li, pt_ref, page0=0, n_sub=n_sub, ppc=ppc, buf_ref=buf, sem=sem)

    @pl.when(s == 0)
    def _():
        dma(seq=s, slot=slot, wait=False)

    @pl.when(s + 1 < pl.num_programs(0))
    def _():
        dma(seq=s + 1, slot=1 - slot, wait=False)

    dma(seq=s, slot=slot, wait=True)

    rows = t_new * MLA_HEADS
    wukt = wukt_ref[...]
    qabs = _dot(qn_ref[0], wukt).astype(BF16)
    qr = qr_ref[0]
    _flash_init(m_ref, l_ref, acc_ref)

    def block(lat_t, mask):
        ckv = lat_t[0:MLA_KV_LORA].astype(BF16)
        kpe = lat_t[MLA_KV_LORA:MLA_LAT]
        kn = _dot(wukt, ckv)
        tk = kn.shape[-1]
        ss = jnp.sum((kn * kn).reshape(MLA_HEADS, MLA_NOPE, tk), axis=1)
        ss = ss + jnp.sum(kpe * kpe, axis=0, keepdims=True)
        rs = lax.rsqrt(ss / MLA_QK + NORM_EPS)
        sc = _dot(qabs, ckv) + _dot(qr, kpe.astype(BF16))
        sc = sc * jnp.concatenate([rs] * t_new, axis=0)
        if mask is not None:
            sc = jnp.where(mask, sc, NEG)
        _flash_step(sc, ckv, m_ref, l_ref, acc_ref, nt=True)

    def main(sub, carry):
        block(buf[slot, sub], None)
        return carry

    lax.fori_loop(0, n_sub, main, 0, unroll=2 if n_sub % 2 == 0 else 1)
    block(tail_ref[0], _tail_mask(rows, MLA_HEADS))

    o_lat = _flash_out(l_ref, acc_ref).astype(BF16)
    o = _dot(o_lat, wuv_ref[...])
    own = lax.broadcasted_iota(jnp.int32, (MLA_HEADS, 512), 1) // MLA_V == \
        lax.broadcasted_iota(jnp.int32, (MLA_HEADS, 512), 0)
    for q in range(t_new):
        blk = jnp.where(own, o[MLA_HEADS * q:MLA_HEADS * (q + 1)], 0.0)
        o_ref[0, q:q + 1, :] = jnp.sum(blk, axis=0, keepdims=True).astype(o_ref.dtype)


def _sub_pages(n_pages, want):
    ppc = want
    while n_pages % ppc:
        ppc //= 2
    return ppc


def _mla_sample(page_table, cache_t, li, qm, lat_new, wukt, wuv, n_seq):
    t_new = qm.shape[0] // n_seq
    rows = t_new * MLA_HEADS
    n_pages = page_table.shape[1]
    page = cache_t.shape[-1]
    ppc = _sub_pages(n_pages, 8)
    n_sub = n_pages // ppc
    q4 = qm.reshape(n_seq, t_new, MLA_HEADS, LANES)
    eye = jnp.eye(MLA_HEADS, dtype=qm.dtype)
    qn = (q4[:, :, :, None, :MLA_NOPE] * eye[None, None, :, :, None]).reshape(n_seq, rows, MLA_HEADS * MLA_NOPE)
    qr = q4[..., MLA_NOPE:MLA_QK].reshape(n_seq, rows, MLA_ROPE)
    tail = _feature_major_tail(lat_new, n_seq)
    seq3 = lambda a: pl.BlockSpec((1,) + a.shape[1:], lambda s, pt: (s, 0, 0))
    full = lambda a: pl.BlockSpec(a.shape, lambda s, pt: (0,) * a.ndim)
    out = pl.pallas_call(
        functools.partial(_mla_sample_kernel, li=li, n_sub=n_sub, ppc=ppc, t_new=t_new),
        grid_spec=pltpu.PrefetchScalarGridSpec(
            num_scalar_prefetch=1, grid=(n_seq,),
            in_specs=[pl.BlockSpec(memory_space=pl.ANY), seq3(qn), seq3(qr), seq3(tail), full(wukt), full(wuv)],
            out_specs=pl.BlockSpec((1, t_new, 512), lambda s, pt: (s, 0, 0)),
            scratch_shapes=[pltpu.VMEM((2, n_sub, MLA_LAT, ppc * page), F32), pltpu.SemaphoreType.DMA((2,)),
                            pltpu.VMEM((rows, LANES), F32), pltpu.VMEM((rows, LANES), F32),
                            pltpu.VMEM((rows, MLA_KV_LORA), F32)]),
        out_shape=jax.ShapeDtypeStruct((n_seq, t_new, 512), BF16),
        compiler_params=_cparams(("arbitrary",)),
        name="mla_sample_attention",
    )(page_table, cache_t, qn, qr, tail, wukt, wuv)
    return out.reshape(n_seq * t_new, 512)


def _dsa_sample_kernel(pt_ref, idx_cache, kv_cache, iq_ref, iw_ref, q_ref, itail_ref, ktail_ref, o_ref,
                       ibuf, kbuf, isem, ksem, key_ref, m_ref, l_ref, acc_ref,
                       *, li, n_sub, ppc, t_new, topk, idx_bits):
    s = pl.program_id(0)
    slot = s % 2
    idma = functools.partial(_page_dma, idx_cache, li, pt_ref, page0=0, n_sub=n_sub, ppc=ppc, buf_ref=ibuf, sem=isem)
    kdma = functools.partial(_page_dma, kv_cache, li, pt_ref, page0=0, n_sub=n_sub, ppc=ppc, buf_ref=kbuf, sem=ksem)

    @pl.when(s == 0)
    def _():
        idma(seq=s, slot=slot, wait=False)
        kdma(seq=s, slot=slot, wait=False)

    @pl.when(s + 1 < pl.num_programs(0))
    def _():
        idma(seq=s + 1, slot=1 - slot, wait=False)
        kdma(seq=s + 1, slot=1 - slot, wait=False)

    idma(seq=s, slot=slot, wait=True)

    rows = t_new * IDX_HEADS
    tk = ibuf.shape[-1]
    n_past = n_sub * tk
    iq = iq_ref[0]
    iw = iw_ref[0] * IDX_SCALE
    qrow = lax.broadcasted_iota(jnp.int32, (8, 1), 0)

    def keys_of(ik_t, admissible):
        w = jnp.maximum(_dot(iq, ik_t.astype(BF16)), 0.0) * iw
        width = w.shape[-1]
        sc = jnp.sum(w.reshape(t_new, IDX_HEADS, width), axis=1)
        sc = jnp.concatenate([sc, jnp.zeros((8 - t_new, width), F32)], axis=0)
        return jnp.where(admissible, _sort_key(sc), INT_MIN)

    for sub in range(n_sub):
        key_ref[:, sub * tk:(sub + 1) * tk] = keys_of(ibuf[slot, sub], qrow < t_new)
    tail_ok = (_lane((8, LANES)) <= qrow) & (qrow < t_new)
    key_ref[:, n_past:n_past + LANES] = keys_of(itail_ref[0], tail_ok)

    def count(f):
        return _row_count(f(key_ref[...], _lane(key_ref.shape), slice(None)))

    thr, bound = _topk_threshold(count, 8, topk, idx_bits)

    kdma(seq=s, slot=slot, wait=True)
    q = q_ref[0]
    _flash_init(m_ref, l_ref, acc_ref)

    def attend(lane0, width, kv_t):
        key = key_ref[:, lane0:lane0 + width]
        sel = _selected(key, lane0 + _lane((8, width)), thr, bound)
        bias = jnp.where(sel, 0.0, NEG)
        bias = jnp.concatenate([jnp.broadcast_to(bias[qq:qq + 1], (DSA_HEADS, width)) for qq in range(t_new)], axis=0)
        sc = _dot(q, kv_t[0:128].astype(BF16)) + bias
        _flash_step(sc, kv_t[128:256].astype(BF16), m_ref, l_ref, acc_ref, nt=True)

    for sub in range(n_sub):
        attend(sub * tk, tk, kbuf[slot, sub])
    attend(n_past, LANES, ktail_ref[0])
    o_ref[0] = _flash_out(l_ref, acc_ref)


def _dsa_sample(page_table, idx_t, kv_t, li, qd, iq, misc, kv_new, topk, n_seq):
    t_new = qd.shape[0] // n_seq
    rows = t_new * DSA_HEADS
    n_pages = page_table.shape[1]
    page = kv_t.shape[-1]
    ppc = _sub_pages(n_pages, 8)
    n_sub = n_pages // ppc
    tk = ppc * page
    idx_bits = int(n_pages * page + LANES - 1).bit_length()
    q = qd.reshape(n_seq, rows, LANES)
    iqr = iq.reshape(n_seq, rows, LANES)[..., :IDX_DIM]
    iw = misc[:, M_IW:M_IW + IDX_HEADS].reshape(n_seq, rows, 1)
    itail = _feature_major_tail(misc[:, 0:IDX_DIM], n_seq)
    ktail = _feature_major_tail(kv_new, n_seq)
    seq3 = lambda a: pl.BlockSpec((1,) + a.shape[1:], lambda s, pt: (s, 0, 0))
    out = pl.pallas_call(
        functools.partial(_dsa_sample_kernel, li=li, n_sub=n_sub, ppc=ppc, t_new=t_new, topk=topk, idx_bits=idx_bits),
        grid_spec=pltpu.PrefetchScalarGridSpec(
            num_scalar_prefetch=1, grid=(n_seq,),
            in_specs=[pl.BlockSpec(memory_space=pl.ANY), pl.BlockSpec(memory_space=pl.ANY),
                      seq3(iqr), seq3(iw), seq3(q), seq3(itail), seq3(ktail)],
            out_specs=pl.BlockSpec((1, rows, LANES), lambda s, pt: (s, 0, 0)),
            scratch_shapes=[pltpu.VMEM((2, n_sub, IDX_DIM, tk), F32), pltpu.VMEM((2, n_sub, 256, tk), F32),
                            pltpu.SemaphoreType.DMA((2,)), pltpu.SemaphoreType.DMA((2,)),
                            pltpu.VMEM((8, n_sub * tk + LANES), jnp.int32)]
            + [pltpu.VMEM((rows, LANES), F32)] * 3),
        out_shape=jax.ShapeDtypeStruct((n_seq, rows, LANES), F32),
        compiler_params=_cparams(("arbitrary",)),
        name="dsa_sample_attention",
    )(page_table, idx_t, kv_t, iqr, iw, q, itail, ktail)
    o = out.reshape(n_seq, t_new, DSA_HEADS, 2, HEAD_DIM)
    heads = jnp.arange(DSA_HEADS)
    o = o[:, :, heads, heads // DSA_GROUP, :]
    return o.reshape(n_seq * t_new, DSA_HEADS * HEAD_DIM).astype(BF16)


def _fox_sample_kernel(pt_ref, kv_cache, lf_cache, q_ref, ktail_ref, ltail_ref, o_ref,
                       kbuf, lbuf, ksem, lsem, m_ref, l_ref, acc_ref, carry_ref, colg_ref,
                       *, li, n_sub, ppc, n_chunk, t_new):
    s, c = pl.program_id(0), pl.program_id(1)
    step = s * n_chunk + c
    slot = step % 2
    cpp = n_sub * ppc

    def dma(seq, chunk, slot, wait):
        page0 = (n_chunk - 1 - chunk) * cpp
        _page_dma(kv_cache, li, pt_ref, seq, page0, n_sub, ppc, kbuf, slot, ksem, wait)
        _page_dma(lf_cache, li, pt_ref, seq, page0, n_sub, ppc, lbuf, slot, lsem, wait)

    @pl.when(step == 0)
    def _():
        dma(s, c, slot, False)

    @pl.when(step + 1 < pl.num_programs(0) * n_chunk)
    def _():
        nxt = step + 1
        dma(nxt // n_chunk, nxt % n_chunk, 1 - slot, False)

    rows = t_new * FOX_HEADS
    q = q_ref[0]
    sb = min(256, kbuf.shape[-1])
    later = (lax.broadcasted_iota(jnp.int32, (sb, sb), 0) > lax.broadcasted_iota(jnp.int32, (sb, sb), 1)).astype(BF16)

    def later_sums(lf_t):
        w = lf_t.shape[-1]
        bw = min(sb, w)
        parts = jnp.concatenate(_split3(lf_t), axis=0)
        carry = carry_ref[...]
        gs = [None] * (w // bw)
        for jb in reversed(range(w // bw)):
            sfx = _dot(parts[:, jb * bw:(jb + 1) * bw], later[0:bw, 0:bw])
            gs[jb] = sfx[0:FOX_HEADS] + sfx[FOX_HEADS:2 * FOX_HEADS] + sfx[2 * FOX_HEADS:3 * FOX_HEADS] + carry
            carry = carry + jnp.sum(lf_t[:, jb * bw:(jb + 1) * bw], axis=-1, keepdims=True)
        carry_ref[...] = carry
        return jnp.concatenate(gs, axis=1) * LOG2E

    def block(k_t, v_t, lf_t, mask, first):
        g = later_sums(lf_t)
        if first:
            colg_ref[...] = jnp.concatenate([g[:, qq:qq + 1] for qq in range(t_new)], axis=0)
        sc = _dot(q, k_t.astype(BF16)) + jnp.concatenate([g] * t_new, axis=0) - colg_ref[...]
        if mask is not None:
            sc = jnp.where(mask, sc, NEG)
        _flash_step(sc, v_t.astype(BF16), m_ref, l_ref, acc_ref, nt=True)

    @pl.when(c == 0)
    def _():
        _flash_init(m_ref, l_ref, acc_ref)
        carry_ref[...] = jnp.zeros(carry_ref.shape, F32)
        block(ktail_ref[0, 0:512], ktail_ref[0, 512:1024], ltail_ref[0], _tail_mask(rows, FOX_HEADS), True)

    dma(s, c, slot, True)
    block(kbuf[slot, 0, 0:512], kbuf[slot, 0, 512:1024], lbuf[slot, 0], None, False)

    @pl.when(c == n_chunk - 1)
    def _():
        o_ref[0] = _flash_out(l_ref, acc_ref)


def _fox_sample(page_table, kv_t, lf_t, li, q, kv_new, lf_new, n_seq):
    t_new = q.shape[0] // n_seq
    rows = t_new * FOX_HEADS
    n_pages = page_table.shape[1]
    page = kv_t.shape[-1]
    ppc = _sub_pages(n_pages, 16)
    n_sub, n_chunk = 1, n_pages // ppc
    tk = ppc * page
    heads = jnp.arange(FOX_HEADS)
    place = (heads[:, None] // 4 == jnp.arange(4)[None, :]).astype(q.dtype)
    qpad = q.reshape(n_seq, t_new, FOX_HEADS, 1, FQ_W)[..., :LANES]
    qbd = (qpad * place[None, None, :, :, None]).reshape(n_seq, rows, 4 * LANES)
    ktail = _feature_major_tail(kv_new, n_seq)
    ltail = _feature_major_tail(lf_new, n_seq)
    seq3 = lambda a: pl.BlockSpec((1,) + a.shape[1:], lambda s, c, pt: (s, 0, 0))
    out = pl.pallas_call(
        functools.partial(_fox_sample_kernel, li=li, n_sub=n_sub, ppc=ppc, n_chunk=n_chunk, t_new=t_new),
        grid_spec=pltpu.PrefetchScalarGridSpec(
            num_scalar_prefetch=1, grid=(n_seq, n_chunk),
            in_specs=[pl.BlockSpec(memory_space=pl.ANY), pl.BlockSpec(memory_space=pl.ANY),
                      seq3(qbd), seq3(ktail), seq3(ltail)],
            out_specs=pl.BlockSpec((1, rows, 512), lambda s, c, pt: (s, 0, 0)),
            scratch_shapes=[pltpu.VMEM((2, n_sub, 1024, tk), F32), pltpu.VMEM((2, n_sub, FOX_HEADS, tk), F32),
                            pltpu.SemaphoreType.DMA((2,)), pltpu.SemaphoreType.DMA((2,)),
                            pltpu.VMEM((rows, LANES), F32), pltpu.VMEM((rows, LANES), F32),
                            pltpu.VMEM((rows, 512), F32),
                            pltpu.VMEM((FOX_HEADS, 1), F32), pltpu.VMEM((rows, 1), F32)]),
        out_shape=jax.ShapeDtypeStruct((n_seq, rows, 512), F32),
        compiler_params=_cparams(("arbitrary", "arbitrary")),
        name="fox_sample_attention",
    )(page_table, kv_t, lf_t, qbd, ktail, ltail)
    o = out.reshape(n_seq, t_new, FOX_HEADS, FOX_KV_HEADS, HEAD_DIM)
    o = o[:, :, heads, heads // 2, :]
    return o.reshape(n_seq * t_new, FOX_HEADS * HEAD_DIM).astype(BF16)


def _prompt_even(xp, t_real, tab, g_mix, ew, topk):
    qm, km, vm, lat, qd, kvd, kdb, iq, misc = _even_project(xp, t_real, tab, g_mix, ew)
    om = _mla_prompt(qm, km, vm)
    od = _dsa_prompt(qd, iq, misc, kdb, topk)
    return om, od, lat, kvd, misc[:, :t_real, 0:IDX_DIM]


def _prompt_odd(xp, t_real, g_mix, ow):
    q, kv, kvb, lf = _odd_project(xp, t_real, g_mix, ow)
    return _fox_prompt(q, kvb), kv, lf


def kernel(x_prompt, x_sample, cache_mla, cache_dsa_kv, cache_dsa_idx, cache_fox_kv, cache_fox_logf, page_table, meta_tokens, g_mix, g_ffn, w_in_even, g_mla_q_lat, w_mla_uq, g_mla_kv_lat, w_mla_uk, w_mla_uv, g_mla_q, g_mla_k, g_dsa_q, g_dsa_k, g_idx_k, w_out_even, w_in_odd, b_fox_f, g_fox_q, g_fox_k, w_out_odd, w_ffn_gate, w_ffn_up, w_ffn_down):
    b, s, d = x_prompt.shape
    depth = g_mix.shape[0]
    t_real = s + N_META
    tp = -(-t_real // LANES) * LANES
    meta = jnp.broadcast_to(meta_tokens.astype(x_prompt.dtype)[None], (b, N_META, d))
    xp = jnp.concatenate([meta, x_prompt, jnp.zeros((b, tp - t_real, d), x_prompt.dtype)], axis=1)
    tab_p = _rope_tables(jnp.arange(tp))
    topk_p = min(TOPK_MAX, s // 4)

    n_seq, t_new, _ = x_sample.shape
    n_rows = n_seq * t_new
    past = page_table.shape[1] * cache_mla.shape[2]
    xs = x_sample.reshape(1, n_rows, d)
    tab_s = _rope_tables(past + jnp.arange(n_rows) % t_new)
    topk_s = min(TOPK_MAX, (past + t_new) // 4)
    mla_t = jnp.swapaxes(cache_mla, 2, 3)
    dkv_t = jnp.transpose(cache_dsa_kv, (0, 1, 3, 4, 5, 2)).reshape(cache_dsa_kv.shape[:2] + (256, -1))
    idx_t = jnp.swapaxes(cache_dsa_idx, 2, 3)
    fkv_t = jnp.transpose(cache_fox_kv, (0, 1, 3, 4, 5, 2)).reshape(cache_fox_kv.shape[:2] + (1024, -1))
    flf_t = jnp.swapaxes(cache_fox_logf, 2, 3)

    mla_p, dkv_p, idx_p, fkv_p, lf_p = [], [], [], [], []
    mla_s, dkv_s, idx_s, fkv_s, lf_s = [], [], [], [], []
    for l in range(depth):
        li = l // 2
        wg, wu, wd = w_ffn_gate[l].astype(BF16), w_ffn_up[l].astype(BF16), w_ffn_down[l].astype(BF16)
        if l % 2 == 0:
            ew = _even_weights(w_in_even[li], g_mla_q_lat[li], w_mla_uq[li], g_mla_kv_lat[li], w_mla_uk[li],
                               w_mla_uv[li], g_mla_q[li], g_mla_k[li], g_dsa_q[li], g_dsa_k[li], g_idx_k[li])
            w_out = w_out_even[li].astype(BF16)
            w_outs = (w_out[:512], w_out[512:])
            om, od, lat, kvd, ik = _prompt_even(xp, t_real, tab_p, g_mix[l], ew, topk_p)
            mla_p.append(lat)
            dkv_p.append(kvd.reshape(b, t_real, 2, DSA_KV_HEADS, HEAD_DIM))
            idx_p.append(ik)
            xp = _mix_ffn(xp, (om, od), w_outs, g_ffn[l], wg, wu, wd)

            qm, _, _, lat, qd, kvd, _, iq, misc = _even_project(xs, n_rows, tab_s, g_mix[l], ew)
            wukt = jnp.transpose(w_mla_uk[li], (1, 2, 0)).reshape(MLA_HEADS * MLA_NOPE, MLA_KV_LORA).astype(BF16)
            om = _mla_sample(page_table, mla_t, li, qm[0], lat[0], wukt, ew[5], n_seq)
            od = _dsa_sample(page_table, idx_t, dkv_t, li, qd[0], iq[0], misc[0], kvd[0], topk_s, n_seq)
            mla_s.append(lat.reshape(n_seq, t_new, MLA_LAT))
            dkv_s.append(kvd.reshape(n_seq, t_new, 2, DSA_KV_HEADS, HEAD_DIM))
            idx_s.append(misc[0, :, 0:IDX_DIM].reshape(n_seq, t_new, IDX_DIM))
            xs = _mix_ffn(xs, (om[None], od[None]), w_outs, g_ffn[l], wg, wu, wd)
        else:
            ow = _odd_weights(w_in_odd[li], b_fox_f[li], g_fox_q[li], g_fox_k[li])
            w_outs = (w_out_odd[li].astype(BF16),)
            of, kv, lf = _prompt_odd(xp, t_real, g_mix[l], ow)
            fkv_p.append(kv.reshape(b, t_real, 2, FOX_KV_HEADS, HEAD_DIM))
            lf_p.append(lf)
            xp = _mix_ffn(xp, (of,), w_outs, g_ffn[l], wg, wu, wd)

            q, kv, _, lf = _odd_project(xs, n_rows, g_mix[l], ow)
            of = _fox_sample(page_table, fkv_t, flf_t, li, q[0], kv[0], lf[0], n_seq)
            fkv_s.append(kv.reshape(n_seq, t_new, 2, FOX_KV_HEADS, HEAD_DIM))
            lf_s.append(lf.reshape(n_seq, t_new, FOX_HEADS))
            xs = _mix_ffn(xs, (of[None],), w_outs, g_ffn[l], wg, wu, wd)
    y_prompt = xp[:, N_META:t_real]
    y_sample = xs.reshape(n_seq, t_new, d)
    return (y_prompt, y_sample, jnp.stack(mla_p), jnp.stack(mla_s), jnp.stack(dkv_p), jnp.stack(dkv_s),
            jnp.stack(idx_p), jnp.stack(idx_s), jnp.stack(fkv_p), jnp.stack(fkv_s), jnp.stack(lf_p), jnp.stack(lf_s))
```
